```python
import math
import jax, jax.numpy as jnp
from jax import lax
import numpy as np

D_MODEL = 4096
BATCH = 1
SEQ = 8192
DEPTH = 4

MIX_W = 3 * D_MODEL // 4
X_W = D_MODEL // 4
X_HEADS = 4
X_HEAD_DIM = X_W // X_HEADS
N_MEM = 256

CONV_CH = MIX_W
CONV_K = 31

NSA_HEAD_DIM = 128
NSA_HEADS = MIX_W // NSA_HEAD_DIM
NSA_KV_HEADS = 4
NSA_GROUP = NSA_HEADS // NSA_KV_HEADS
CMP_STRIDE = 16
CMP_LEN = 2 * CMP_STRIDE
CMP_HIDDEN = 256
SEL_BLOCK = 64
N_SEL = 16
WINDOW = 512
Q_BLOCK = 128

D_FF = 4 * D_MODEL

NSA_QW = NSA_HEADS * NSA_HEAD_DIM
NSA_KVW = NSA_KV_HEADS * NSA_HEAD_DIM
NSA_GW = NSA_HEADS * 3
NSA_COLS = NSA_QW + 6 * NSA_KVW + NSA_GW
NSA_SPLITS = tuple(NSA_QW + k * NSA_KVW for k in range(7))
CF_COLS = 2 * CONV_CH
N_CF = (DEPTH + 1) // 2
N_NSA = DEPTH // 2

EPS = 1e-6
NEG = -1e30
FORCE = 1e4

kernel_name = "hybrid_conformer_nsa_memxattn_trunk"


def rmsnorm(x, g):
    xf = x.astype(jnp.float32)
    y = xf * lax.rsqrt(jnp.mean(xf * xf, axis=-1, keepdims=True) + EPS)
    return (y * g.astype(jnp.float32)).astype(x.dtype)


def layernorm(x, g, b):
    xf = x.astype(jnp.float32)
    mu = jnp.mean(xf, axis=-1, keepdims=True)
    var = jnp.mean(jnp.square(xf - mu), axis=-1, keepdims=True)
    y = (xf - mu) * lax.rsqrt(var + EPS)
    return y * g.astype(jnp.float32) + b.astype(jnp.float32)


def masked_softmax(s, m):
    return jax.nn.softmax(jnp.where(m, s, NEG), axis=-1) * m


def conformer_conv(u, conv_w, conv_b, ln_g, ln_b):
    dt = u.dtype
    a, b = jnp.split(u, 2, axis=-1)
    z = a * jax.nn.sigmoid(b)
    zp = jnp.pad(z, ((0, 0), (CONV_K - 1, 0), (0, 0)))
    y = lax.conv_general_dilated(
        zp, conv_w[:, None, :], window_strides=(1,), padding='VALID',
        dimension_numbers=('NWC', 'WIO', 'NWC'), feature_group_count=CONV_CH)
    y = y + conv_b
    y = layernorm(y, ln_g, ln_b)
    return jax.nn.silu(y).astype(dt)


def compress_blocks(t, pos, w1, w2):
    B, S, G, dh = t.shape
    c = t.reshape(B, S // CMP_STRIDE, CMP_STRIDE, G, dh)
    blk = jnp.concatenate([c[:, :-1], c[:, 1:]], axis=2)
    blk = blk + pos[None, None, :, None, :]
    ncmp = blk.shape[1]
    flat = blk.transpose(0, 1, 3, 2, 4).reshape(B, ncmp, G, CMP_LEN * dh)
    hid = jax.nn.silu(jnp.einsum('bngf,fh->bngh', flat, w1))
    return jnp.einsum('bngh,hd->bngd', hid, w2)


def nsa_mixer(h, q_gain, k_gain, cmp_pos, cmp_w1, cmp_w2):
    B, S, _ = h.shape
    dt = h.dtype
    G, R, dh = NSA_KV_HEADS, NSA_GROUP, NSA_HEAD_DIM
    scale = dh ** -0.5
    q, kc, vc, ks, vs, kw, vw, g = jnp.split(h, NSA_SPLITS, axis=-1)
    q = rmsnorm(q.reshape(B, S, NSA_HEADS, dh), q_gain).reshape(B, S, G, R, dh)
    kv = lambda t: t.reshape(B, S, G, dh)
    gates = jax.nn.sigmoid(g).reshape(B, S, G, R, 3)

    kc_ = rmsnorm(compress_blocks(kv(kc), cmp_pos[0], cmp_w1[0], cmp_w2[0]), k_gain[0])
    vc_ = compress_blocks(kv(vc), cmp_pos[1], cmp_w1[1], cmp_w2[1])
    ncmp = kc_.shape[1]
    cmp_end = jnp.arange(ncmp) * CMP_STRIDE + CMP_LEN - 1

    nslc = S // SEL_BLOCK
    n_sel = min(N_SEL, nslc)
    ks_blk = rmsnorm(kv(ks), k_gain[1]).reshape(B, nslc, SEL_BLOCK, G, dh).transpose(0, 3, 1, 2, 4)
    vs_blk = kv(vs).reshape(B, nslc, SEL_BLOCK, G, dh).transpose(0, 3, 1, 2, 4)
    ci_start = jnp.arange(ncmp) * CMP_STRIDE
    jb = jnp.arange(nslc)
    overlap = ((ci_start[:, None] < (jb[None, :] + 1) * SEL_BLOCK) &
               (ci_start[:, None] + CMP_LEN > jb[None, :] * SEL_BLOCK)).astype(jnp.float32)
    gather = jax.vmap(jax.vmap(lambda blk, ix: blk[ix]))

    pad = ((0, 0), (WINDOW, 0), (0, 0), (0, 0))
    kwp = jnp.pad(rmsnorm(kv(kw), k_gain[2]), pad)
    vwp = jnp.pad(kv(vw), pad)

    def q_block(ci):
        start = ci * Q_BLOCK
        t = start + jnp.arange(Q_BLOCK)
        qc = lax.dynamic_slice_in_dim(q, start, Q_BLOCK, axis=1)

        s = jnp.einsum('bqgrd,bngd->bgrqn', qc, kc_).astype(jnp.float32) * scale
        m = cmp_end[None, :] <= t[:, None]
        p = masked_softmax(s, m)
        o_c = jnp.einsum('bgrqn,bngd->bqgrd', p.astype(dt), vc_)

        imp = jnp.einsum('bgrqn,nj->bgqj', p, overlap)
        cur = t // SEL_BLOCK
        valid = jb[None, :] * SEL_BLOCK <= t[:, None]
        forced = (jb[None, :] == 0) | (jb[None, :] == cur[:, None]) | (jb[None, :] == cur[:, None] - 1)
        score = jnp.where(valid, imp + FORCE * forced.astype(jnp.float32), NEG)
        _, idx = lax.top_k(score, n_sel)
        ksel = gather(ks_blk, idx)
        vsel = gather(vs_blk, idx)
        s = jnp.einsum('bqgrd,bgqnkd->bgrqnk', qc, ksel).astype(jnp.float32) * scale
        kpos = idx[..., None] * SEL_BLOCK + jnp.arange(SEL_BLOCK)
        m = (kpos <= t[None, None, :, None, None])[:, :, None]
        shp = s.shape
        p = masked_softmax(s.reshape(shp[:4] + (n_sel * SEL_BLOCK,)),
                           m.reshape(m.shape[:4] + (n_sel * SEL_BLOCK,))).reshape(shp)
        o_s = jnp.einsum('bgrqnk,bgqnkd->bqgrd', p.astype(dt), vsel)

        kwin = lax.dynamic_slice_in_dim(kwp, start, Q_BLOCK + WINDOW, axis=1)
        vwin = lax.dynamic_slice_in_dim(vwp, start, Q_BLOCK + WINDOW, axis=1)
        kpos = start - WINDOW + jnp.arange(Q_BLOCK + WINDOW)
        m = ((kpos[None, :] <= t[:, None]) & (kpos[None, :] > t[:, None] - WINDOW) &
             (kpos[None, :] >= 0))
        s = jnp.einsum('bqgrd,bkgd->bgrqk', qc, kwin).astype(jnp.float32) * scale
        p = masked_softmax(s, m)
        o_w = jnp.einsum('bgrqk,bkgd->bqgrd', p.astype(dt), vwin)

        gc = lax.dynamic_slice_in_dim(gates, start, Q_BLOCK, axis=1)
        o = gc[..., 0:1] * o_c + gc[..., 1:2] * o_s + gc[..., 2:3] * o_w
        return o.reshape(B, Q_BLOCK, NSA_QW)

    out = lax.map(q_block, jnp.arange(S // Q_BLOCK))
    return out.transpose(1, 0, 2, 3).reshape(B, S, NSA_QW)


def memory_cross_attention(qx, mem, mem_norm_g, w_mem_kv, q_gain, k_gain):
    B, S, _ = qx.shape
    dt = qx.dtype
    q = rmsnorm(qx.reshape(B, S, X_HEADS, X_HEAD_DIM), q_gain)
    kv = rmsnorm(mem, mem_norm_g) @ w_mem_kv
    k, v = jnp.split(kv, 2, axis=-1)
    M = mem.shape[1]
    k = rmsnorm(k.reshape(B, M, X_HEADS, X_HEAD_DIM), k_gain)
    v = v.reshape(B, M, X_HEADS, X_HEAD_DIM)
    s = jnp.einsum('bshd,bmhd->bhsm', q, k).astype(jnp.float32) * (X_HEAD_DIM ** -0.5)
    p = jax.nn.softmax(s, axis=-1)
    o = jnp.einsum('bhsm,bmhd->bshd', p.astype(dt), v)
    return o.reshape(B, S, X_W)


def setup_inputs(seed: int = 0) -> dict:
    key = jax.random.key(seed)
    ks = jax.random.split(key, 22)
    n = lambda k, shape, s: jax.random.normal(k, shape, jnp.float32) * s
    gain = lambda k, shape: 1.0 + 0.02 * jax.random.normal(k, shape, jnp.float32)
    return {
        "x": n(ks[0], (BATCH, SEQ, D_MODEL), 1.0),
        "mem": n(ks[1], (BATCH, N_MEM, D_MODEL), 1.0),
        "attn_norm": gain(ks[2], (DEPTH, D_MODEL)),
        "mem_norm": gain(ks[3], (DEPTH, D_MODEL)),
        "w_mem_kv": n(ks[4], (DEPTH, D_MODEL, 2 * X_W), D_MODEL ** -0.5),
        "xq_gain": gain(ks[5], (DEPTH, X_HEAD_DIM)),
        "xk_gain": gain(ks[6], (DEPTH, X_HEAD_DIM)),
        "w_out": n(ks[7], (DEPTH, D_MODEL, D_MODEL), D_MODEL ** -0.5),
        "mlp_norm": gain(ks[8], (DEPTH, D_MODEL)),
        "w_up": n(ks[9], (DEPTH, D_MODEL, D_FF), D_MODEL ** -0.5),
        "w_down": n(ks[10], (DEPTH, D_FF, D_MODEL), D_FF ** -0.5),
        "cf_w_in": n(ks[11], (N_CF, D_MODEL, CF_COLS + X_W), D_MODEL ** -0.5),
        "cf_conv_w": n(ks[12], (N_CF, CONV_K, CONV_CH), CONV_K ** -0.5),
        "cf_conv_b": n(ks[13], (N_CF, CONV_CH), 0.01),
        "cf_ln_g": gain(ks[14], (N_CF, CONV_CH)),
        "cf_ln_b": n(ks[15], (N_CF, CONV_CH), 0.01),
        "nsa_w_in": n(ks[16], (N_NSA, D_MODEL, NSA_COLS + X_W), D_MODEL ** -0.5),
        "nsa_q_gain": gain(ks[17], (N_NSA, NSA_HEAD_DIM)),
        "nsa_k_gain": gain(ks[18], (N_NSA, 3, NSA_HEAD_DIM)),
        "nsa_cmp_pos": n(ks[19], (N_NSA, 2, CMP_LEN, NSA_HEAD_DIM), 0.02),
        "nsa_cmp_w1": n(ks[20], (N_NSA, 2, CMP_LEN * NSA_HEAD_DIM, CMP_HIDDEN), (CMP_LEN * NSA_HEAD_DIM) ** -0.5),
        "nsa_cmp_w2": n(ks[21], (N_NSA, 2, CMP_HIDDEN, NSA_HEAD_DIM), CMP_HIDDEN ** -0.5),
    }


def reference(x, mem, attn_norm, mem_norm, w_mem_kv, xq_gain, xk_gain, w_out, mlp_norm,
              w_up, w_down, cf_w_in, cf_conv_w, cf_conv_b, cf_ln_g, cf_ln_b,
              nsa_w_in, nsa_q_gain, nsa_k_gain, nsa_cmp_pos, nsa_cmp_w1, nsa_cmp_w2):
    for i in range(DEPTH):
        j = i // 2
        hn = rmsnorm(x, attn_norm[i])
        if i % 2 == 0:
            h = hn @ cf_w_in[j]
            mix = conformer_conv(h[..., :CF_COLS], cf_conv_w[j], cf_conv_b[j], cf_ln_g[j], cf_ln_b[j])
            qx = h[..., CF_COLS:]
        else:
            h = hn @ nsa_w_in[j]
            mix = nsa_mixer(h[..., :NSA_COLS], nsa_q_gain[j], nsa_k_gain[j],
                            nsa_cmp_pos[j], nsa_cmp_w1[j], nsa_cmp_w2[j])
            qx = h[..., NSA_COLS:]
        xo = memory_cross_attention(qx, mem, mem_norm[i], w_mem_kv[i], xq_gain[i], xk_gain[i])
        x = x + jnp.concatenate([mix, xo], axis=-1) @ w_out[i]
        hn = rmsnorm(x, mlp_norm[i])
        x = x + jnp.square(jax.nn.relu(hn @ w_up[i])) @ w_down[i]
    return x
```

```python
import functools

import jax
import jax.numpy as jnp
from jax import lax
from jax.experimental import pallas as pl
from jax.experimental.pallas import tpu as pltpu

D_MODEL = 4096
DEPTH = 4
MIX_W = 3 * D_MODEL // 4
X_W = D_MODEL // 4
X_HEADS = 4
X_HEAD_DIM = X_W // X_HEADS
CONV_CH = MIX_W
CONV_K = 31
NSA_HEAD_DIM = 128
NSA_HEADS = MIX_W // NSA_HEAD_DIM
NSA_KV_HEADS = 4
NSA_GROUP = NSA_HEADS // NSA_KV_HEADS
CMP_STRIDE = 16
CMP_LEN = 2 * CMP_STRIDE
CMP_HIDDEN = 256
SEL_BLOCK = 64
N_SEL = 16
WINDOW = 512
D_FF = 4 * D_MODEL
NSA_QW = NSA_HEADS * NSA_HEAD_DIM
NSA_KVW = NSA_KV_HEADS * NSA_HEAD_DIM
NSA_GW = NSA_HEADS * 3
NSA_COLS = NSA_QW + 6 * NSA_KVW + NSA_GW
EPS = 1e-6
NEG = -1e30
FORCE = 1e4

LANES = 128
VMEM_LIMIT = 56 * 1024 * 1024

M_FLOOR = -1e20

NSA_QX_OFF = NSA_QW + 6 * NSA_KVW
NSA_GATE_OFF = NSA_QX_OFF + X_W
NSA_PCOLS = NSA_GATE_OFF + NSA_KV_HEADS * LANES
GATE_STRIDE = 8
CF_QX_OFF = 2 * CONV_CH

F32 = jnp.float32
BF16 = jnp.bfloat16
_NT = (((1,), (1,)), ((), ()))


def _params(sem):
    return pltpu.CompilerParams(dimension_semantics=sem, vmem_limit_bytes=VMEM_LIMIT)


def _rmsnorm_kernel(x_ref, g_ref, o_ref):
    x = x_ref[...]
    ms = jnp.mean(x * x, axis=-1, keepdims=True)
    o_ref[...] = (x * lax.rsqrt(ms + EPS) * g_ref[...]).astype(o_ref.dtype)


def _rmsnorm(x, g, tm):
    m, d = x.shape
    return pl.pallas_call(
        _rmsnorm_kernel,
        grid=(m // tm,),
        in_specs=[pl.BlockSpec((tm, d), lambda i: (i, 0)),
                  pl.BlockSpec((1, d), lambda i: (0, 0))],
        out_specs=pl.BlockSpec((tm, d), lambda i: (i, 0)),
        out_shape=jax.ShapeDtypeStruct((m, d), BF16),
        compiler_params=_params(("parallel",)),
        name="rmsnorm",
    )(x, g.reshape(1, d))


def _mm_kernel(*refs, nk, epilogue):
    if epilogue == "residual":
        a_ref, w_ref, r_ref, o_ref = refs[:4]
        scratch = refs[4:]
    else:
        a_ref, w_ref, o_ref = refs[:3]
        r_ref = None
        scratch = refs[3:]

    def finish(acc):
        if epilogue == "residual":
            acc = acc + r_ref[...]
        elif epilogue == "relu2":
            acc = jnp.square(jnp.maximum(acc, 0.0))
        o_ref[...] = acc.astype(o_ref.dtype)

    part = jnp.dot(a_ref[...], w_ref[...], preferred_element_type=F32)
    if nk == 1:
        finish(part)
        return
    acc_ref, = scratch
    k = pl.program_id(2)

    @pl.when(k == 0)
    def _():
        acc_ref[...] = part

    @pl.when(jnp.logical_and(k > 0, k < nk - 1))
    def _():
        acc_ref[...] += part

    @pl.when(k == nk - 1)
    def _():
        finish(acc_ref[...] + part)


def _matmul(a, w, *, tm, tn, tk, out_dtype, epilogue=None, residual=None):
    m, kdim = a.shape
    _, n = w.shape
    nk = kdim // tk
    in_specs = [pl.BlockSpec((tm, tk), lambda i, j, k: (i, k)),
                pl.BlockSpec((tk, tn), lambda i, j, k: (k, j))]
    args = [a, w]
    if epilogue == "residual":
        in_specs.append(pl.BlockSpec((tm, tn), lambda i, j, k: (i, j)))
        args.append(residual)
    scratch = [pltpu.VMEM((tm, tn), F32)] if nk > 1 else []
    return pl.pallas_call(
        functools.partial(_mm_kernel, nk=nk, epilogue=epilogue),
        grid=(m // tm, n // tn, nk),
        in_specs=in_specs,
        out_specs=pl.BlockSpec((tm, tn), lambda i, j, k: (i, j)),
        out_shape=jax.ShapeDtypeStruct((m, n), out_dtype),
        scratch_shapes=scratch,
        compiler_params=_params(("parallel", "parallel", "arbitrary")),
        name="matmul_" + (epilogue or "plain"),
    )(*args)


CONV_TS = 256
CONV_HALO = 32
CONV_CW = 128


def _conv_kernel(a_ref, b_ref, w_ref, cb_ref, g_ref, bb_ref, o_ref, zbuf, ybuf):
    i = pl.program_id(0)
    ts = CONV_TS

    @pl.when(i == 0)
    def _():
        zbuf[0:CONV_HALO, :] = jnp.zeros((CONV_HALO, CONV_CH), F32)

    @pl.when(i > 0)
    def _():
        zbuf[0:CONV_HALO, :] = zbuf[ts:ts + CONV_HALO, :]

    zbuf[CONV_HALO:, :] = a_ref[...] * jax.nn.sigmoid(b_ref[...])

    base = CONV_HALO - (CONV_K - 1)

    def chunk(c, carry):
        cols = pl.ds(pl.multiple_of(c * CONV_CW, CONV_CW), CONV_CW)
        acc = jnp.zeros((ts, CONV_CW), F32)
        for k in range(CONV_K):
            acc = acc + w_ref[k:k + 1, cols] * zbuf[base + k:base + k + ts, cols]
        ybuf[:, cols] = acc + cb_ref[:, cols]
        return carry

    lax.fori_loop(0, CONV_CH // CONV_CW, chunk, 0)

    y = ybuf[...]
    mu = jnp.mean(y, axis=-1, keepdims=True)
    yc = y - mu
    var = jnp.mean(yc * yc, axis=-1, keepdims=True)
    yn = yc * lax.rsqrt(var + EPS) * g_ref[...] + bb_ref[...]
    o_ref[...] = (yn * jax.nn.sigmoid(yn)).astype(o_ref.dtype)


def _conformer_conv(h, conv_w, conv_b, ln_g, ln_b):
    s = h.shape[0]
    c = CONV_CH
    wpad = jnp.zeros((CONV_HALO, c), F32).at[:CONV_K].set(conv_w)
    row = lambda v: v.reshape(1, c)
    full = lambda r: pl.BlockSpec((r, c), lambda i: (0, 0))
    return pl.pallas_call(
        _conv_kernel,
        grid=(s // CONV_TS,),
        in_specs=[pl.BlockSpec((CONV_TS, c), lambda i: (i, 0)),
                  pl.BlockSpec((CONV_TS, c), lambda i: (i, 1)),
                  full(CONV_HALO), full(1), full(1), full(1)],
        out_specs=pl.BlockSpec((CONV_TS, c), lambda i: (i, 0)),
        out_shape=jax.ShapeDtypeStruct((s, D_MODEL), BF16),
        scratch_shapes=[pltpu.VMEM((CONV_TS + CONV_HALO, c), F32),
                        pltpu.VMEM((CONV_TS, c), F32)],
        compiler_params=_params(("arbitrary",)),
        name="conformer_conv",
    )(h, h, wpad, row(conv_b), row(ln_g), row(ln_b))


XATTN_TS = 512


def _head_rmsnorm(x, gain, scale=1.0):
    ms = jnp.mean(x * x, axis=-1, keepdims=True)
    return x * (lax.rsqrt(ms + EPS) * scale) * gain


def _xattn_kernel(q_ref, kv_ref, qg_ref, kg_ref, mix_ref, o_ref):
    del mix_ref
    dh = X_HEAD_DIM
    for h in range(X_HEADS):
        q = _head_rmsnorm(q_ref[:, h * dh:(h + 1) * dh], qg_ref[...], dh ** -0.5).astype(BF16)
        k = _head_rmsnorm(kv_ref[:, h * dh:(h + 1) * dh], kg_ref[...]).astype(BF16)
        v = kv_ref[:, X_W + h * dh:X_W + (h + 1) * dh].astype(BF16)
        s = lax.dot_general(q, k, _NT, preferred_element_type=F32)
        p = jnp.exp(s - jnp.max(s, axis=-1, keepdims=True))
        l = jnp.sum(p, axis=-1, keepdims=True)
        o = jnp.dot(p.astype(BF16), v, preferred_element_type=F32) / l
        o_ref[:, h * dh:(h + 1) * dh] = o.astype(o_ref.dtype)


def _memory_xattn(h, qx_off, kv, q_gain, k_gain, mixbuf):
    s = h.shape[0]
    n_mem = kv.shape[0]
    qblk = qx_off // X_W
    oblk = MIX_W // X_W
    return pl.pallas_call(
        _xattn_kernel,
        grid=(s // XATTN_TS,),
        in_specs=[pl.BlockSpec((XATTN_TS, X_W), lambda i: (i, qblk)),
                  pl.BlockSpec((n_mem, 2 * X_W), lambda i: (0, 0)),
                  pl.BlockSpec((1, X_HEAD_DIM), lambda i: (0, 0)),
                  pl.BlockSpec((1, X_HEAD_DIM), lambda i: (0, 0)),
                  pl.BlockSpec(memory_space=pl.ANY)],
        out_specs=pl.BlockSpec((XATTN_TS, X_W), lambda i: (i, oblk)),
        out_shape=jax.ShapeDtypeStruct(mixbuf.shape, mixbuf.dtype),
        input_output_aliases={4: 0},
        compiler_params=_params(("parallel",)),
        name="memory_xattn",
    )(h, kv, q_gain.reshape(1, -1), k_gain.reshape(1, -1), mixbuf)


KVPREP_TS = 512


def _kvprep_kernel(x_ref, g_ref, o_ref):
    dh = NSA_HEAD_DIM
    for part in range(4):
        for g in range(NSA_KV_HEADS):
            lo = part * NSA_KVW + g * dh
            x = x_ref[:, lo:lo + dh]
            if part % 2 == 0:
                x = _head_rmsnorm(x, g_ref[part // 2:part // 2 + 1, :])
            o_ref[:, lo:lo + dh] = x.astype(o_ref.dtype)


def _nsa_kvprep(h, k_gain):
    s = h.shape[0]
    w = 4 * NSA_KVW
    blk = (NSA_QW + 2 * NSA_KVW) // w
    return pl.pallas_call(
        _kvprep_kernel,
        grid=(s // KVPREP_TS,),
        in_specs=[pl.BlockSpec((KVPREP_TS, w), lambda i: (i, blk)),
                  pl.BlockSpec((2, NSA_HEAD_DIM), lambda i: (0, 0))],
        out_specs=pl.BlockSpec((KVPREP_TS, w), lambda i: (i, 0)),
        out_shape=jax.ShapeDtypeStruct((s, w), BF16),
        compiler_params=_params(("parallel",)),
        name="nsa_kvprep",
    )(h, k_gain[1:3])


def _compress_kernel(c_ref, pa_ref, pb_ref, w1_ref, w2_ref, kg_ref, o_ref):
    half = CMP_STRIDE * NSA_HEAD_DIM
    c = c_ref[...]
    first = jnp.dot((c + pa_ref[...]).astype(BF16), w1_ref[0:half, :], preferred_element_type=F32)
    second = jnp.dot((c + pb_ref[...]).astype(BF16), w1_ref[half:2 * half, :], preferred_element_type=F32)
    nchunk = c.shape[0]
    hid = first + pltpu.roll(second, nchunk - 1, 0)
    hid = hid * jax.nn.sigmoid(hid)
    out = jnp.dot(hid.astype(BF16), w2_ref[...], preferred_element_type=F32)
    normed = _head_rmsnorm(out, kg_ref[...])
    is_key = pl.program_id(0) == 0
    o_ref[...] = jnp.where(is_key, normed, out).astype(o_ref.dtype)


def _nsa_compress(h, cmp_pos, cmp_w1, cmp_w2, k_gain):
    s = h.shape[0]
    g, dh = NSA_KV_HEADS, NSA_HEAD_DIM
    nchunk = s // CMP_STRIDE
    half = CMP_STRIDE * dh
    c = h[:, NSA_QW:NSA_QW + 2 * NSA_KVW].reshape(s, 2, g, dh).transpose(1, 2, 0, 3).reshape(2, g, nchunk, half)
    pos = cmp_pos.reshape(2, 2, 1, half)
    return pl.pallas_call(
        _compress_kernel,
        grid=(2, g),
        in_specs=[pl.BlockSpec((None, None, nchunk, half), lambda a, b: (a, b, 0, 0)),
                  pl.BlockSpec((None, None, 1, half), lambda a, b: (a, 0, 0, 0)),
                  pl.BlockSpec((None, None, 1, half), lambda a, b: (a, 1, 0, 0)),
                  pl.BlockSpec((None, 2 * half, CMP_HIDDEN), lambda a, b: (a, 0, 0)),
                  pl.BlockSpec((None, CMP_HIDDEN, dh), lambda a, b: (a, 0, 0)),
                  pl.BlockSpec((1, dh), lambda a, b: (0, 0))],
        out_specs=pl.BlockSpec((None, None, nchunk, dh), lambda a, b: (a, b, 0, 0)),
        out_shape=jax.ShapeDtypeStruct((2, g, nchunk, dh), BF16),
        compiler_params=_params(("parallel", "parallel")),
        name="nsa_compress",
    )(c, pos, pos, cmp_w1.astype(BF16), cmp_w2.astype(BF16), k_gain[0:1])


NSA_TQ = 128
NSA_TK = 512
WIN_KEYS = WINDOW + NSA_TQ


def _softmax_parts(s3, bias):
    s3 = s3 + bias[None]
    m = jnp.maximum(jnp.max(s3, axis=-1, keepdims=True), M_FLOOR)
    p = jnp.exp(s3 - m)
    return p, jnp.sum(p, axis=-1, keepdims=True)


def _safe_inv(l):
    return jnp.where(l > 0.0, 1.0 / l, 0.0)


def _nsa_kernel(q_ref, gate_ref, kc_ref, vc_ref, ks_ref, vs_ref, kw_ref, vw_ref, qg_ref, o_ref, st_ref):
    r_, tq, dh, tk = NSA_GROUP, NSA_TQ, NSA_HEAD_DIM, NSA_TK
    i = pl.program_id(1)
    start = i * tq
    ncmp = kc_ref.shape[0]
    nslc = LANES
    t_col = start + lax.broadcasted_iota(jnp.int32, (tq, 1), 0)

    q = jnp.concatenate(
        [_head_rmsnorm(q_ref[:, r * dh:(r + 1) * dh], qg_ref[...], dh ** -0.5).astype(BF16)
         for r in range(r_)], axis=0)

    n_row = lax.broadcasted_iota(jnp.int32, (1, ncmp), 1)
    bias_c = jnp.where(n_row * CMP_STRIDE + (CMP_LEN - 1) <= t_col, 0.0, NEG)
    s = lax.dot_general(q, kc_ref[...], _NT, preferred_element_type=F32)
    p, l = _softmax_parts(s.reshape(r_, tq, ncmp), bias_c)
    p = p * _safe_inv(l)
    o_c = jnp.dot(p.reshape(r_ * tq, ncmp).astype(BF16), vc_ref[...], preferred_element_type=F32)

    psum = jnp.sum(p, axis=0)
    n_col = lax.broadcasted_iota(jnp.int32, (ncmp, 1), 0) * CMP_STRIDE
    j_row = lax.broadcasted_iota(jnp.int32, (1, nslc), 1)
    overlap = jnp.where((n_col < (j_row + 1) * SEL_BLOCK) & (n_col + CMP_LEN > j_row * SEL_BLOCK),
                        1.0, 0.0).astype(BF16)
    p_hi = psum.astype(BF16)
    p_lo = (psum - p_hi.astype(F32)).astype(BF16)
    imp = (jnp.dot(p_hi, overlap, preferred_element_type=F32)
           + jnp.dot(p_lo, overlap, preferred_element_type=F32))
    cur = t_col >> 6
    valid = j_row * SEL_BLOCK <= t_col
    forced = (j_row == 0) | (j_row == cur) | (j_row == cur - 1)
    score = jnp.where(valid, imp + jnp.where(forced, FORCE, 0.0), NEG)
    score_t = score.T
    st_ref[...] = score_t
    j_col = lax.broadcasted_iota(jnp.int32, (nslc, 1), 0)

    def rank_body(jp, rank):
        other = st_ref[pl.ds(jp, 1), :]
        beats = (other > score_t) | ((other == score_t) & (jp < j_col))
        return rank + jnp.where(beats, 1.0, 0.0)

    rank = lax.fori_loop(0, nslc, rank_body, jnp.zeros((nslc, tq), F32), unroll=8)
    sel = jnp.where(rank < float(N_SEL), 1.0, 0.0).T.astype(BF16)

    def sel_body(c, carry):
        m, l, acc = carry
        rows = pl.ds(pl.multiple_of(c * tk, tk), tk)
        s = lax.dot_general(q, ks_ref[rows, :], _NT, preferred_element_type=F32).reshape(r_, tq, tk)
        kl = lax.broadcasted_iota(jnp.int32, (1, tk), 1)
        expand = jnp.where(j_col == c * (tk // SEL_BLOCK) + (kl >> 6), 1.0, 0.0).astype(BF16)
        selx = jnp.dot(sel, expand, preferred_element_type=F32)
        bias = jnp.where((selx > 0.5) & (c * tk + kl <= t_col), 0.0, NEG)
        s = s + bias[None]
        m_new = jnp.maximum(m, jnp.max(s, axis=-1, keepdims=True))
        alpha = jnp.exp(m - m_new)
        p = jnp.exp(s - m_new)
        l = alpha * l + jnp.sum(p, axis=-1, keepdims=True)
        pv = jnp.dot(p.reshape(r_ * tq, tk).astype(BF16), vs_ref[rows, :], preferred_element_type=F32)
        acc = alpha * acc + pv.reshape(r_, tq, dh)
        return m_new, l, acc

    n_tiles = (start + tq + tk - 1) // tk
    init = (jnp.full((r_, tq, 1), M_FLOOR, F32), jnp.zeros((r_, tq, 1), F32), jnp.zeros((r_, tq, dh), F32))
    _, l_s, acc_s = lax.fori_loop(0, n_tiles, sel_body, init)
    o_s = acc_s * _safe_inv(l_s)

    lo = pl.multiple_of(jnp.maximum(start - WINDOW, 0), tq)
    kpos = lo + lax.broadcasted_iota(jnp.int32, (1, WIN_KEYS), 1)
    bias_w = jnp.where((kpos <= t_col) & (kpos > t_col - WINDOW), 0.0, NEG)
    s = lax.dot_general(q, kw_ref[pl.ds(lo, WIN_KEYS), :], _NT, preferred_element_type=F32)
    p, l = _softmax_parts(s.reshape(r_, tq, WIN_KEYS), bias_w)
    o_w = jnp.dot(p.reshape(r_ * tq, WIN_KEYS).astype(BF16), vw_ref[pl.ds(lo, WIN_KEYS), :],
                  preferred_element_type=F32).reshape(r_, tq, dh) * _safe_inv(l)

    gates = jax.nn.sigmoid(gate_ref[...])
    o_c = o_c.reshape(r_, tq, dh)
    for r in range(r_):
        gate = lambda b: gates[:, b * GATE_STRIDE + r:b * GATE_STRIDE + r + 1]
        o = gate(0) * o_c[r] + gate(1) * o_s[r] + gate(2) * o_w[r]
        o_ref[:, r * dh:(r + 1) * dh] = o.astype(o_ref.dtype)


def _nsa_attention(h, cmp_kv, kvn, q_gain):
    s = h.shape[0]
    g, dh = NSA_KV_HEADS, NSA_HEAD_DIM
    qw = NSA_GROUP * dh
    ncmp = cmp_kv.shape[2]
    gate_blk = NSA_GATE_OFF // LANES
    kv_spec = lambda part: pl.BlockSpec((s, dh), lambda a, i: (0, part * g + a))
    return pl.pallas_call(
        _nsa_kernel,
        grid=(g, s // NSA_TQ),
        in_specs=[pl.BlockSpec((NSA_TQ, qw), lambda a, i: (i, a)),
                  pl.BlockSpec((NSA_TQ, LANES), lambda a, i: (i, gate_blk + a)),
                  pl.BlockSpec((None, None, ncmp, dh), lambda a, i: (0, a, 0, 0)),
                  pl.BlockSpec((None, None, ncmp, dh), lambda a, i: (1, a, 0, 0)),
                  kv_spec(0), kv_spec(1), kv_spec(2), kv_spec(3),
                  pl.BlockSpec((1, dh), lambda a, i: (0, 0))],
        out_specs=pl.BlockSpec((NSA_TQ, qw), lambda a, i: (i, a)),
        out_shape=jax.ShapeDtypeStruct((s, D_MODEL), BF16),
        scratch_shapes=[pltpu.VMEM((LANES, NSA_TQ), F32)],
        compiler_params=_params(("parallel", "arbitrary")),
        name="nsa_attention",
    )(h, h, cmp_kv, cmp_kv, kvn, kvn, kvn, kvn, q_gain.reshape(1, dh))


def _nsa_permute_w_in(w):
    gate_cols = jnp.zeros((w.shape[0], NSA_KV_HEADS, LANES), w.dtype)
    gsrc = w[:, NSA_QX_OFF:NSA_QX_OFF + NSA_GW].reshape(w.shape[0], NSA_KV_HEADS, NSA_GROUP, 3)
    for b in range(3):
        gate_cols = gate_cols.at[:, :, b * GATE_STRIDE:b * GATE_STRIDE + NSA_GROUP].set(gsrc[..., b])
    return jnp.concatenate([w[:, :NSA_QX_OFF], w[:, NSA_COLS:], gate_cols.reshape(w.shape[0], -1)], axis=1)


MM_TM = 1024
NORM_TM = 256


def kernel(x, mem, attn_norm, mem_norm, w_mem_kv, xq_gain, xk_gain, w_out, mlp_norm, w_up, w_down,
           cf_w_in, cf_conv_w, cf_conv_b, cf_ln_g, cf_ln_b,
           nsa_w_in, nsa_q_gain, nsa_k_gain, nsa_cmp_pos, nsa_cmp_w1, nsa_cmp_w2):
    b, s, d = x.shape
    assert b == 1 and d == D_MODEL and s % MM_TM == 0 and s // SEL_BLOCK == LANES
    x2 = x.reshape(s, d)
    mem2 = mem.reshape(mem.shape[1], d)
    n_mem = mem2.shape[0]
    for i in range(DEPTH):
        j = i // 2
        hn = _rmsnorm(x2, attn_norm[i], NORM_TM)
        if i % 2 == 0:
            h = _matmul(hn, cf_w_in[j].astype(BF16), tm=MM_TM, tn=1024, tk=d, out_dtype=F32)
            mixbuf = _conformer_conv(h, cf_conv_w[j], cf_conv_b[j], cf_ln_g[j], cf_ln_b[j])
            qx_off = CF_QX_OFF
        else:
            h = _matmul(hn, _nsa_permute_w_in(nsa_w_in[j]).astype(BF16), tm=MM_TM, tn=768, tk=d, out_dtype=F32)
            cmp_kv = _nsa_compress(h, nsa_cmp_pos[j], nsa_cmp_w1[j], nsa_cmp_w2[j], nsa_k_gain[j])
            kvn = _nsa_kvprep(h, nsa_k_gain[j])
            mixbuf = _nsa_attention(h, cmp_kv, kvn, nsa_q_gain[j])
            qx_off = NSA_QX_OFF
        memn = _rmsnorm(mem2, mem_norm[i], n_mem)
        kv = _matmul(memn, w_mem_kv[i].astype(BF16), tm=n_mem, tn=1024, tk=d, out_dtype=F32)
        mixbuf = _memory_xattn(h, qx_off, kv, xq_gain[i], xk_gain[i], mixbuf)
        x2 = _matmul(mixbuf, w_out[i].astype(BF16), tm=MM_TM, tn=1024, tk=d, out_dtype=F32,
                     epilogue="residual", residual=x2)
        hn = _rmsnorm(x2, mlp_norm[i], NORM_TM)
        u = _matmul(hn, w_up[i].astype(BF16), tm=MM_TM, tn=1024, tk=d, out_dtype=BF16, epilogue="relu2")
        x2 = _matmul(u, w_down[i].astype(BF16), tm=MM_TM, tn=1024, tk=2048, out_dtype=F32,
                     epilogue="residual", residual=x2)
    return x2.reshape(b, s, d)
```

```python
import functools
import math

import jax
import jax.numpy as jnp
from jax import lax
from jax.experimental import pallas as pl
from jax.experimental.pallas import tpu as pltpu

D_MODEL = 4096
DEPTH = 4
MIX_W = 3 * D_MODEL // 4
X_W = D_MODEL // 4
X_HEADS = 4
X_HEAD_DIM = X_W // X_HEADS
CONV_CH = MIX_W
CONV_K = 31
NSA_HEAD_DIM = 128
NSA_HEADS = MIX_W // NSA_HEAD_DIM
NSA_KV_HEADS = 4
NSA_GROUP = NSA_HEADS // NSA_KV_HEADS
CMP_STRIDE = 16
CMP_LEN = 2 * CMP_STRIDE
CMP_HIDDEN = 256
SEL_BLOCK = 64
N_SEL = 16
WINDOW = 512
D_FF = 4 * D_MODEL
NSA_QW = NSA_HEADS * NSA_HEAD_DIM
NSA_KVW = NSA_KV_HEADS * NSA_HEAD_DIM
NSA_GW = NSA_HEADS * 3
NSA_COLS = NSA_QW + 6 * NSA_KVW + NSA_GW
EPS = 1e-6
NEG = -1e30
FORCE = 1e4

LANES = 128
SUBLANES = 8
VMEM_LIMIT = 56 * 1024 * 1024

M_FLOOR = -1e20
LOG2E = math.log2(math.e)

NSA_MAIN_COLS = NSA_QW + 6 * NSA_KVW
NSA_TAIL_COLS = X_W + NSA_KV_HEADS * LANES
GATE_STRIDE = 8
CF_QX_OFF = 2 * CONV_CH

F32 = jnp.float32
BF16 = jnp.bfloat16
_NT = (((1,), (1,)), ((), ()))


def _params(sem):
    return pltpu.CompilerParams(dimension_semantics=sem, vmem_limit_bytes=VMEM_LIMIT)


def _prenorm_kernel(x_ref, g_ref, xg_ref, ssq_ref):
    x = x_ref[...]
    xg_ref[...] = (x * g_ref[...]).astype(xg_ref.dtype)
    ssq_ref[...] = jnp.broadcast_to(jnp.sum(x * x, axis=-1, keepdims=True), ssq_ref.shape)


def _prenorm(x, g, tm):
    m, d = x.shape
    return pl.pallas_call(
        _prenorm_kernel,
        grid=(m // tm,),
        in_specs=[pl.BlockSpec((tm, d), lambda i: (i, 0)),
                  pl.BlockSpec((1, d), lambda i: (0, 0))],
        out_specs=[pl.BlockSpec((tm, d), lambda i: (i, 0)),
                   pl.BlockSpec((tm, LANES), lambda i: (i, 0))],
        out_shape=[jax.ShapeDtypeStruct((m, d), BF16), jax.ShapeDtypeStruct((m, LANES), F32)],
        compiler_params=_params(("parallel",)),
        name="prenorm",
    )(x, g.reshape(1, d))


def _mm_kernel(*refs, nk, kdim, rowscale, epilogue, emit_norm):
    refs = list(refs)
    a_ref, w_ref = refs[:2]
    pos = 2
    ssq_in = r_ref = g_ref = xg_ref = ssq_out = None
    if rowscale:
        ssq_in = refs[pos]
        pos += 1
    if epilogue == "residual":
        r_ref = refs[pos]
        pos += 1
        if emit_norm:
            g_ref = refs[pos]
            pos += 1
    o_ref = refs[pos]
    pos += 1
    if emit_norm:
        xg_ref, ssq_out = refs[pos:pos + 2]
        pos += 2
    scratch = refs[pos:]

    def finish(acc):
        if rowscale:
            acc = acc * lax.rsqrt(ssq_in[:, 0:1] * (1.0 / kdim) + EPS)
        if epilogue == "residual":
            acc = acc + r_ref[...]
        elif epilogue == "relu2":
            acc = jnp.square(jnp.maximum(acc, 0.0))
        o_ref[...] = acc.astype(o_ref.dtype)
        if emit_norm:
            xg_ref[...] = (acc * g_ref[...]).astype(xg_ref.dtype)
            part = jnp.broadcast_to(jnp.sum(acc * acc, axis=-1, keepdims=True), ssq_out.shape)
            j = pl.program_id(1)

            @pl.when(j == 0)
            def _():
                ssq_out[...] = part

            @pl.when(j > 0)
            def _():
                ssq_out[...] += part

    part = jnp.dot(a_ref[...], w_ref[...], preferred_element_type=F32)
    if nk == 1:
        finish(part)
        return
    acc_ref, = scratch
    k = pl.program_id(2)

    @pl.when(k == 0)
    def _():
        acc_ref[...] = part

    @pl.when(jnp.logical_and(k > 0, k < nk - 1))
    def _():
        acc_ref[...] += part

    @pl.when(k == nk - 1)
    def _():
        finish(acc_ref[...] + part)


def _matmul(a, w, *, tm, tn, tk, out_dtype, n=None, ssq=None, epilogue=None, residual=None, next_gain=None):
    m, kdim = a.shape
    n = w.shape[1] if n is None else n
    nk = kdim // tk
    emit_norm = next_gain is not None
    in_specs = [pl.BlockSpec((tm, tk), lambda i, j, k: (i, k)),
                pl.BlockSpec((tk, tn), lambda i, j, k: (k, j))]
    args = [a, w]
    if ssq is not None:
        in_specs.append(pl.BlockSpec((tm, LANES), lambda i, j, k: (i, 0)))
        args.append(ssq)
    if epilogue == "residual":
        in_specs.append(pl.BlockSpec((tm, tn), lambda i, j, k: (i, j)))
        args.append(residual)
        if emit_norm:
            in_specs.append(pl.BlockSpec((1, tn), lambda i, j, k: (0, j)))
            args.append(next_gain.reshape(1, n))
    out_specs = [pl.BlockSpec((tm, tn), lambda i, j, k: (i, j))]
    out_shape = [jax.ShapeDtypeStruct((m, n), out_dtype)]
    if emit_norm:
        out_specs += [pl.BlockSpec((tm, tn), lambda i, j, k: (i, j)),
                      pl.BlockSpec((tm, LANES), lambda i, j, k: (i, 0))]
        out_shape += [jax.ShapeDtypeStruct((m, n), BF16), jax.ShapeDtypeStruct((m, LANES), F32)]
    scratch = [pltpu.VMEM((tm, tn), F32)] if nk > 1 else []
    out = pl.pallas_call(
        functools.partial(_mm_kernel, nk=nk, kdim=kdim, rowscale=ssq is not None, epilogue=epilogue,
                          emit_norm=emit_norm),
        grid=(m // tm, n // tn, nk),
        in_specs=in_specs,
        out_specs=out_specs,
        out_shape=out_shape,
        scratch_shapes=scratch,
        compiler_params=_params(("parallel", "arbitrary", "arbitrary")),
        name="matmul_" + (epilogue or "plain"),
    )(*args)
    return out if emit_norm else out[0]


CONV_TS = 256
CONV_HALO = 32
CONV_CW = 128


def _conv_kernel(a_ref, b_ref, w_ref, cb_ref, g_ref, bb_ref, o_ref, zbuf, ybuf):
    i = pl.program_id(0)
    ts = CONV_TS

    @pl.when(i == 0)
    def _():
        zbuf[0:CONV_HALO, :] = jnp.zeros((CONV_HALO, CONV_CH), F32)
        zbuf[CONV_HALO + ts:, :] = jnp.zeros((SUBLANES, CONV_CH), F32)

    @pl.when(i > 0)
    def _():
        zbuf[0:CONV_HALO, :] = zbuf[ts:ts + CONV_HALO, :]

    zbuf[CONV_HALO:CONV_HALO + ts, :] = a_ref[...] * jax.nn.sigmoid(b_ref[...])

    base = CONV_HALO - (CONV_K - 1)

    def chunk(c, carry):
        cols = pl.ds(pl.multiple_of(c * CONV_CW, CONV_CW), CONV_CW)
        acc = jnp.zeros((ts, CONV_CW), F32)
        for phase in range(SUBLANES):
            part = jnp.zeros((ts + SUBLANES, CONV_CW), F32)
            for k in range(CONV_K):
                if (base + k) % SUBLANES == phase:
                    off = base + k - phase
                    part = part + w_ref[k:k + 1, cols] * zbuf[off:off + ts + SUBLANES, cols]
            acc = acc + part[phase:phase + ts, :]
        ybuf[:, cols] = acc + cb_ref[:, cols]
        return carry

    lax.fori_loop(0, CONV_CH // CONV_CW, chunk, 0)

    y = ybuf[...]
    mu = jnp.mean(y, axis=-1, keepdims=True)
    yc = y - mu
    var = jnp.mean(yc * yc, axis=-1, keepdims=True)
    yn = yc * lax.rsqrt(var + EPS) * g_ref[...] + bb_ref[...]
    o_ref[...] = (yn * jax.nn.sigmoid(yn)).astype(o_ref.dtype)


def _conformer_conv(h, conv_w, conv_b, ln_g, ln_b):
    s = h.shape[0]
    c = CONV_CH
    wpad = jnp.zeros((CONV_HALO, c), F32).at[:CONV_K].set(conv_w)
    row = lambda v: v.reshape(1, c)
    full = lambda r: pl.BlockSpec((r, c), lambda i: (0, 0))
    return pl.pallas_call(
        _conv_kernel,
        grid=(s // CONV_TS,),
        in_specs=[pl.BlockSpec((CONV_TS, c), lambda i: (i, 0)),
                  pl.BlockSpec((CONV_TS, c), lambda i: (i, 1)),
                  full(CONV_HALO), full(1), full(1), full(1)],
        out_specs=pl.BlockSpec((CONV_TS, c), lambda i: (i, 0)),
        out_shape=jax.ShapeDtypeStruct((s, D_MODEL), BF16),
        scratch_shapes=[pltpu.VMEM((CONV_TS + CONV_HALO + SUBLANES, c), F32),
                        pltpu.VMEM((CONV_TS, c), F32)],
        compiler_params=_params(("arbitrary",)),
        name="conformer_conv",
    )(h, h, wpad, row(conv_b), row(ln_g), row(ln_b))


XATTN_TS = 512


def _head_rmsnorm(x, gain, scale=1.0):
    ms = jnp.mean(x * x, axis=-1, keepdims=True)
    return x * (lax.rsqrt(ms + EPS) * scale) * gain


def _xattn_kernel(q_ref, kv_ref, qg_ref, kg_ref, mix_ref, o_ref):
    del mix_ref
    dh = X_HEAD_DIM
    for h in range(X_HEADS):
        q = _head_rmsnorm(q_ref[:, h * dh:(h + 1) * dh], qg_ref[...], dh ** -0.5 * LOG2E).astype(BF16)
        k = _head_rmsnorm(kv_ref[:, h * dh:(h + 1) * dh], kg_ref[...]).astype(BF16)
        v = kv_ref[:, X_W + h * dh:X_W + (h + 1) * dh].astype(BF16)
        s = lax.dot_general(q, k, _NT, preferred_element_type=F32)
        p = jnp.exp2(s - jnp.max(s, axis=-1, keepdims=True))
        l = jnp.sum(p, axis=-1, keepdims=True)
        o = jnp.dot(p.astype(BF16), v, preferred_element_type=F32) / l
        o_ref[:, h * dh:(h + 1) * dh] = o.astype(o_ref.dtype)


def _memory_xattn(h, qx_off, kv, q_gain, k_gain, mixbuf):
    s = h.shape[0]
    n_mem = kv.shape[0]
    qblk = qx_off // X_W
    oblk = MIX_W // X_W
    return pl.pallas_call(
        _xattn_kernel,
        grid=(s // XATTN_TS,),
        in_specs=[pl.BlockSpec((XATTN_TS, X_W), lambda i: (i, qblk)),
                  pl.BlockSpec((n_mem, 2 * X_W), lambda i: (0, 0)),
                  pl.BlockSpec((1, X_HEAD_DIM), lambda i: (0, 0)),
                  pl.BlockSpec((1, X_HEAD_DIM), lambda i: (0, 0)),
                  pl.BlockSpec(memory_space=pl.ANY)],
        out_specs=pl.BlockSpec((XATTN_TS, X_W), lambda i: (i, oblk)),
        out_shape=jax.ShapeDtypeStruct(mixbuf.shape, mixbuf.dtype),
        input_output_aliases={4: 0},
        compiler_params=_params(("parallel",)),
        name="memory_xattn",
    )(h, kv, q_gain.reshape(1, -1), k_gain.reshape(1, -1), mixbuf)


KVPREP_TS = 512


def _kvprep_kernel(x_ref, g_ref, o_ref):
    dh = NSA_HEAD_DIM
    for part in range(4):
        for g in range(NSA_KV_HEADS):
            lo = part * NSA_KVW + g * dh
            x = x_ref[:, lo:lo + dh]
            if part % 2 == 0:
                x = _head_rmsnorm(x, g_ref[part // 2:part // 2 + 1, :])
            o_ref[:, lo:lo + dh] = x.astype(o_ref.dtype)


def _nsa_kvprep(h, k_gain):
    s = h.shape[0]
    w = 4 * NSA_KVW
    blk = (NSA_QW + 2 * NSA_KVW) // w
    return pl.pallas_call(
        _kvprep_kernel,
        grid=(s // KVPREP_TS,),
        in_specs=[pl.BlockSpec((KVPREP_TS, w), lambda i: (i, blk)),
                  pl.BlockSpec((2, NSA_HEAD_DIM), lambda i: (0, 0))],
        out_specs=pl.BlockSpec((KVPREP_TS, w), lambda i: (i, 0)),
        out_shape=jax.ShapeDtypeStruct((s, w), BF16),
        compiler_params=_params(("parallel",)),
        name="nsa_kvprep",
    )(h, k_gain[1:3])


def _compress_kernel(c_ref, pa_ref, pb_ref, w1_ref, w2_ref, kg_ref, o_ref):
    half = CMP_STRIDE * NSA_HEAD_DIM
    c = c_ref[...]
    first = jnp.dot((c + pa_ref[...]).astype(BF16), w1_ref[0:half, :], preferred_element_type=F32)
    second = jnp.dot((c + pb_ref[...]).astype(BF16), w1_ref[half:2 * half, :], preferred_element_type=F32)
    nchunk = c.shape[0]
    hid = first + pltpu.roll(second, nchunk - 1, 0)
    hid = hid * jax.nn.sigmoid(hid)
    out = jnp.dot(hid.astype(BF16), w2_ref[...], preferred_element_type=F32)
    normed = _head_rmsnorm(out, kg_ref[...])
    is_key = pl.program_id(0) == 0
    o_ref[...] = jnp.where(is_key, normed, out).astype(o_ref.dtype)


def _nsa_compress(h, cmp_pos, cmp_w1, cmp_w2, k_gain):
    s = h.shape[0]
    g, dh = NSA_KV_HEADS, NSA_HEAD_DIM
    nchunk = s // CMP_STRIDE
    half = CMP_STRIDE * dh
    c = h[:, NSA_QW:NSA_QW + 2 * NSA_KVW].reshape(s, 2, g, dh).transpose(1, 2, 0, 3).reshape(2, g, nchunk, half)
    pos = cmp_pos.reshape(2, 2, 1, half)
    return pl.pallas_call(
        _compress_kernel,
        grid=(2, g),
        in_specs=[pl.BlockSpec((None, None, nchunk, half), lambda a, b: (a, b, 0, 0)),
                  pl.BlockSpec((None, None, 1, half), lambda a, b: (a, 0, 0, 0)),
                  pl.BlockSpec((None, None, 1, half), lambda a, b: (a, 1, 0, 0)),
                  pl.BlockSpec((None, 2 * half, CMP_HIDDEN), lambda a, b: (a, 0, 0)),
                  pl.BlockSpec((None, CMP_HIDDEN, dh), lambda a, b: (a, 0, 0)),
                  pl.BlockSpec((1, dh), lambda a, b: (0, 0))],
        out_specs=pl.BlockSpec((None, None, nchunk, dh), lambda a, b: (a, b, 0, 0)),
        out_shape=jax.ShapeDtypeStruct((2, g, nchunk, dh), BF16),
        compiler_params=_params(("parallel", "parallel")),
        name="nsa_compress",
    )(c, pos, pos, cmp_w1.astype(BF16), cmp_w2.astype(BF16), k_gain[0:1])


NSA_TQ = 128
NSA_TK = 512
WIN_KEYS = WINDOW + NSA_TQ


def _softmax_parts(s3, bias):
    s3 = s3 + bias[None]
    m = jnp.maximum(jnp.max(s3, axis=-1, keepdims=True), M_FLOOR)
    p = jnp.exp2(s3 - m)
    return p, jnp.sum(p, axis=-1, keepdims=True)


def _safe_inv(l):
    return jnp.where(l > 0.0, 1.0 / l, 0.0)


def _top_n_membership(score_t):
    nblk = score_t.shape[0]
    j_col = lax.broadcasted_iota(jnp.int32, (nblk, 1), 0).astype(F32)
    sel = jnp.zeros(score_t.shape, F32)
    s = score_t
    for _ in range(N_SEL):
        top = jnp.max(s, axis=0, keepdims=True)
        first = jnp.min(jnp.where(s == top, j_col, float(nblk)), axis=0, keepdims=True)
        hit = j_col == first
        sel = jnp.where(hit, 1.0, sel)
        s = jnp.where(hit, -jnp.inf, s)
    return sel


def _nsa_kernel(q_ref, gate_ref, kc_ref, vc_ref, ks_ref, vs_ref, kw_ref, vw_ref, qg_ref, o_ref):
    r_, tq, dh, tk = NSA_GROUP, NSA_TQ, NSA_HEAD_DIM, NSA_TK
    i = pl.program_id(1)
    start = i * tq
    ncmp = kc_ref.shape[0]
    nslc = LANES
    t_col = start + lax.broadcasted_iota(jnp.int32, (tq, 1), 0)

    q = jnp.concatenate(
        [_head_rmsnorm(q_ref[:, r * dh:(r + 1) * dh], qg_ref[...], dh ** -0.5 * LOG2E).astype(BF16)
         for r in range(r_)], axis=0)

    n_row = lax.broadcasted_iota(jnp.int32, (1, ncmp), 1)
    bias_c = jnp.where(n_row * CMP_STRIDE + (CMP_LEN - 1) <= t_col, 0.0, NEG)
    s = lax.dot_general(q, kc_ref[...], _NT, preferred_element_type=F32)
    p, l = _softmax_parts(s.reshape(r_, tq, ncmp), bias_c)
    p = p * _safe_inv(l)
    o_c = jnp.dot(p.reshape(r_ * tq, ncmp).astype(BF16), vc_ref[...], preferred_element_type=F32)

    psum = jnp.sum(p, axis=0)
    n_col = lax.broadcasted_iota(jnp.int32, (ncmp, 1), 0) * CMP_STRIDE
    j_row = lax.broadcasted_iota(jnp.int32, (1, nslc), 1)
    overlap = jnp.where((n_col < (j_row + 1) * SEL_BLOCK) & (n_col + CMP_LEN > j_row * SEL_BLOCK),
                        1.0, 0.0).astype(BF16)
    p_hi = psum.astype(BF16)
    p_lo = (psum - p_hi.astype(F32)).astype(BF16)
    imp = (jnp.dot(p_hi, overlap, preferred_element_type=F32)
           + jnp.dot(p_lo, overlap, preferred_element_type=F32))
    cur = t_col >> 6
    valid = j_row * SEL_BLOCK <= t_col
    forced = (j_row == 0) | (j_row == cur) | (j_row == cur - 1)
    score = jnp.where(valid, imp + jnp.where(forced, FORCE, 0.0), NEG)
    sel = _top_n_membership(score.T).T.astype(BF16)
    j_col = lax.broadcasted_iota(jnp.int32, (nslc, 1), 0)

    def sel_body(c, carry):
        m, l, acc = carry
        rows = pl.ds(pl.multiple_of(c * tk, tk), tk)
        s = lax.dot_general(q, ks_ref[rows, :], _NT, preferred_element_type=F32).reshape(r_, tq, tk)
        kl = lax.broadcasted_iota(jnp.int32, (1, tk), 1)
        expand = jnp.where(j_col == c * (tk // SEL_BLOCK) + (kl >> 6), 1.0, 0.0).astype(BF16)
        selx = jnp.dot(sel, expand, preferred_element_type=F32)
        bias = jnp.where((selx > 0.5) & (c * tk + kl <= t_col), 0.0, NEG)
        s = s + bias[None]
        m_new = jnp.maximum(m, jnp.max(s, axis=-1, keepdims=True))
        alpha = jnp.exp2(m - m_new)
        p = jnp.exp2(s - m_new)
        l = alpha * l + jnp.sum(p, axis=-1, keepdims=True)
        pv = jnp.dot(p.reshape(r_ * tq, tk).astype(BF16), vs_ref[rows, :], preferred_element_type=F32)
        acc = alpha * acc + pv.reshape(r_, tq, dh)
        return m_new, l, acc

    n_tiles = (start + tq + tk - 1) // tk
    init = (jnp.full((r_, tq, 1), M_FLOOR, F32), jnp.zeros((r_, tq, 1), F32), jnp.zeros((r_, tq, dh), F32))
    _, l_s, acc_s = lax.fori_loop(0, n_tiles, sel_body, init)
    o_s = acc_s * _safe_inv(l_s)

    lo = pl.multiple_of(jnp.maximum(start - WINDOW, 0), tq)
    kpos = lo + lax.broadcasted_iota(jnp.int32, (1, WIN_KEYS), 1)
    bias_w = jnp.where((kpos <= t_col) & (kpos > t_col - WINDOW), 0.0, NEG)
    s = lax.dot_general(q, kw_ref[pl.ds(lo, WIN_KEYS), :], _NT, preferred_element_type=F32)
    p, l = _softmax_parts(s.reshape(r_, tq, WIN_KEYS), bias_w)
    o_w = jnp.dot(p.reshape(r_ * tq, WIN_KEYS).astype(BF16), vw_ref[pl.ds(lo, WIN_KEYS), :],
                  preferred_element_type=F32).reshape(r_, tq, dh) * _safe_inv(l)

    gates = jax.nn.sigmoid(gate_ref[...])
    o_c = o_c.reshape(r_, tq, dh)
    for r in range(r_):
        gate = lambda b: gates[:, b * GATE_STRIDE + r:b * GATE_STRIDE + r + 1]
        o = gate(0) * o_c[r] + gate(1) * o_s[r] + gate(2) * o_w[r]
        o_ref[:, r * dh:(r + 1) * dh] = o.astype(o_ref.dtype)


def _nsa_attention(h, h_tail, cmp_kv, kvn, q_gain):
    s = h.shape[0]
    g, dh = NSA_KV_HEADS, NSA_HEAD_DIM
    qw = NSA_GROUP * dh
    ncmp = cmp_kv.shape[2]
    gate_blk = X_W // LANES
    kv_spec = lambda part: pl.BlockSpec((s, dh), lambda a, i: (0, part * g + a))
    return pl.pallas_call(
        _nsa_kernel,
        grid=(g, s // NSA_TQ),
        in_specs=[pl.BlockSpec((NSA_TQ, qw), lambda a, i: (i, a)),
                  pl.BlockSpec((NSA_TQ, LANES), lambda a, i: (i, gate_blk + a)),
                  pl.BlockSpec((None, None, ncmp, dh), lambda a, i: (0, a, 0, 0)),
                  pl.BlockSpec((None, None, ncmp, dh), lambda a, i: (1, a, 0, 0)),
                  kv_spec(0), kv_spec(1), kv_spec(2), kv_spec(3),
                  pl.BlockSpec((1, dh), lambda a, i: (0, 0))],
        out_specs=pl.BlockSpec((NSA_TQ, qw), lambda a, i: (i, a)),
        out_shape=jax.ShapeDtypeStruct((s, D_MODEL), BF16),
        compiler_params=_params(("parallel", "arbitrary")),
        name="nsa_attention",
    )(h, h_tail, cmp_kv, cmp_kv, kvn, kvn, kvn, kvn, q_gain.reshape(1, dh))


def _nsa_tail_weight(w):
    gate_cols = jnp.zeros((w.shape[0], NSA_KV_HEADS, LANES), w.dtype)
    gsrc = w[:, NSA_MAIN_COLS:NSA_MAIN_COLS + NSA_GW].reshape(w.shape[0], NSA_KV_HEADS, NSA_GROUP, 3)
    for b in range(3):
        gate_cols = gate_cols.at[:, :, b * GATE_STRIDE:b * GATE_STRIDE + NSA_GROUP].set(gsrc[..., b])
    return jnp.concatenate([w[:, NSA_COLS:], gate_cols.reshape(w.shape[0], -1)], axis=1)


MM_TM = 1024
NORM_TM = 256


def kernel(x, mem, attn_norm, mem_norm, w_mem_kv, xq_gain, xk_gain, w_out, mlp_norm, w_up, w_down,
           cf_w_in, cf_conv_w, cf_conv_b, cf_ln_g, cf_ln_b,
           nsa_w_in, nsa_q_gain, nsa_k_gain, nsa_cmp_pos, nsa_cmp_w1, nsa_cmp_w2):
    b, s, d = x.shape
    assert b == 1 and d == D_MODEL and s % MM_TM == 0 and s // SEL_BLOCK == LANES
    x2 = x.reshape(s, d)
    mem2 = mem.reshape(mem.shape[1], d)
    n_mem = mem2.shape[0]
    mm = functools.partial(_matmul, tm=MM_TM, tk=d)
    xg, ssq = _prenorm(x2, attn_norm[0], NORM_TM)
    for i in range(DEPTH):
        j = i // 2
        if i % 2 == 0:
            h = mm(xg, cf_w_in[j].astype(BF16), tn=1024, out_dtype=F32, ssq=ssq)
            mixbuf = _conformer_conv(h, cf_conv_w[j], cf_conv_b[j], cf_ln_g[j], cf_ln_b[j])
            h_x, qx_off = h, CF_QX_OFF
        else:
            h = mm(xg, nsa_w_in[j].astype(BF16), tn=768, n=NSA_MAIN_COLS, out_dtype=F32, ssq=ssq)
            h_x = mm(xg, _nsa_tail_weight(nsa_w_in[j]).astype(BF16), tn=768, out_dtype=F32, ssq=ssq)
            cmp_kv = _nsa_compress(h, nsa_cmp_pos[j], nsa_cmp_w1[j], nsa_cmp_w2[j], nsa_k_gain[j])
            kvn = _nsa_kvprep(h, nsa_k_gain[j])
            mixbuf = _nsa_attention(h, h_x, cmp_kv, kvn, nsa_q_gain[j])
            qx_off = 0
        memg, mem_ssq = _prenorm(mem2, mem_norm[i], n_mem)
        kv = _matmul(memg, w_mem_kv[i].astype(BF16), tm=n_mem, tn=1024, tk=d, out_dtype=F32, ssq=mem_ssq)
        mixbuf = _memory_xattn(h_x, qx_off, kv, xq_gain[i], xk_gain[i], mixbuf)
        x2, xg, ssq = mm(mixbuf, w_out[i].astype(BF16), tn=512, out_dtype=F32,
                         epilogue="residual", residual=x2, next_gain=mlp_norm[i])
        u = mm(xg, w_up[i].astype(BF16), tn=1024, out_dtype=BF16, ssq=ssq, epilogue="relu2")
        down = functools.partial(_matmul, u, w_down[i].astype(BF16), tm=MM_TM, tn=1024, tk=2048, out_dtype=F32,
                                 epilogue="residual", residual=x2)
        if i + 1 < DEPTH:
            x2, xg, ssq = down(next_gain=attn_norm[i + 1])
        else:
            x2 = down()
    return x2.reshape(b, s, d)
```

```python
import functools
import math

import jax
import jax.numpy as jnp
from jax import lax
from jax.experimental import pallas as pl
from jax.experimental.pallas import tpu as pltpu

D_MODEL = 4096
DEPTH = 4
MIX_W = 3 * D_MODEL // 4
X_W = D_MODEL // 4
X_HEADS = 4
X_HEAD_DIM = X_W // X_HEADS
CONV_CH = MIX_W
CONV_K = 31
NSA_HEAD_DIM = 128
NSA_HEADS = MIX_W // NSA_HEAD_DIM
NSA_KV_HEADS = 4
NSA_GROUP = NSA_HEADS // NSA_KV_HEADS
CMP_STRIDE = 16
CMP_LEN = 2 * CMP_STRIDE
CMP_HIDDEN = 256
SEL_BLOCK = 64
N_SEL = 16
WINDOW = 512
D_FF = 4 * D_MODEL
NSA_QW = NSA_HEADS * NSA_HEAD_DIM
NSA_KVW = NSA_KV_HEADS * NSA_HEAD_DIM
NSA_GW = NSA_HEADS * 3
NSA_COLS = NSA_QW + 6 * NSA_KVW + NSA_GW
EPS = 1e-6
NEG = -1e30
FORCE = 1e4

LANES = 128
SUBLANES = 8
VMEM_LIMIT = 56 * 1024 * 1024

M_FLOOR = -1e20
LOG2E = math.log2(math.e)

NSA_MAIN_COLS = NSA_QW + 6 * NSA_KVW
NSA_TAIL_COLS = X_W + NSA_KV_HEADS * LANES
GATE_STRIDE = 8
CF_QX_OFF = 2 * CONV_CH

F32 = jnp.float32
BF16 = jnp.bfloat16
_NT = (((1,), (1,)), ((), ()))


def _params(sem):
    return pltpu.CompilerParams(dimension_semantics=sem, vmem_limit_bytes=VMEM_LIMIT)


def _prenorm_kernel(x_ref, g_ref, xg_ref, ssq_ref):
    x = x_ref[...]
    xg_ref[...] = (x * g_ref[...]).astype(xg_ref.dtype)
    ssq_ref[...] = jnp.broadcast_to(jnp.sum(x * x, axis=-1, keepdims=True), ssq_ref.shape)


def _prenorm(x, g, tm):
    m, d = x.shape
    return pl.pallas_call(
        _prenorm_kernel,
        grid=(m // tm,),
        in_specs=[pl.BlockSpec((tm, d), lambda i: (i, 0)),
                  pl.BlockSpec((1, d), lambda i: (0, 0))],
        out_specs=[pl.BlockSpec((tm, d), lambda i: (i, 0)),
                   pl.BlockSpec((tm, LANES), lambda i: (i, 0))],
        out_shape=[jax.ShapeDtypeStruct((m, d), BF16), jax.ShapeDtypeStruct((m, LANES), F32)],
        compiler_params=_params(("parallel",)),
        name="prenorm",
    )(x, g.reshape(1, d))


def _mm_kernel(*refs, nk, kdim, two_a, rowscale, epilogue, emit_norm):
    refs = list(refs)
    a_ref, w_ref = refs[:2]
    pos = 2
    a2_ref = ssq_in = r_ref = g_ref = xg_ref = ssq_out = None
    if two_a:
        a2_ref = refs[pos]
        pos += 1
    if rowscale:
        ssq_in = refs[pos]
        pos += 1
    if epilogue == "residual":
        r_ref = refs[pos]
        pos += 1
        if emit_norm:
            g_ref = refs[pos]
            pos += 1
    o_ref = refs[pos]
    pos += 1
    if emit_norm:
        xg_ref, ssq_out = refs[pos:pos + 2]
        pos += 2
    scratch = refs[pos:]

    def finish(acc):
        if rowscale:
            acc = acc * lax.rsqrt(ssq_in[:, 0:1] * (1.0 / kdim) + EPS)
        if epilogue == "residual":
            acc = acc + r_ref[...]
        elif epilogue == "relu2":
            acc = jnp.square(jnp.maximum(acc, 0.0))
        o_ref[...] = acc.astype(o_ref.dtype)
        if emit_norm:
            xg_ref[...] = (acc * g_ref[...]).astype(xg_ref.dtype)
            part = jnp.broadcast_to(jnp.sum(acc * acc, axis=-1, keepdims=True), ssq_out.shape)
            j = pl.program_id(1)

            @pl.when(j == 0)
            def _():
                ssq_out[...] = part

            @pl.when(j > 0)
            def _():
                ssq_out[...] += part

    if two_a:
        k1 = a_ref.shape[1]
        part = (jnp.dot(a_ref[...], w_ref[0:k1, :].astype(BF16), preferred_element_type=F32)
                + jnp.dot(a2_ref[...], w_ref[k1:, :].astype(BF16), preferred_element_type=F32))
    else:
        part = jnp.dot(a_ref[...], w_ref[...].astype(BF16), preferred_element_type=F32)
    if nk == 1:
        finish(part)
        return
    acc_ref, = scratch
    k = pl.program_id(2)

    @pl.when(k == 0)
    def _():
        acc_ref[...] = part

    @pl.when(jnp.logical_and(k > 0, k < nk - 1))
    def _():
        acc_ref[...] += part

    @pl.when(k == nk - 1)
    def _():
        finish(acc_ref[...] + part)


def _matmul(a, w, *, tm, tn, tk, out_dtype, layer=None, n=None, a2=None, ssq=None, epilogue=None,
            residual=None, next_gain=None):
    m, kdim = a.shape
    n = w.shape[-1] if n is None else n
    if layer is None:
        w_spec = pl.BlockSpec((tk, tn), lambda i, j, k: (k, j))
    else:
        w_spec = pl.BlockSpec((None, tk, tn), lambda i, j, k: (layer, k, j))
    args = [a, w]
    if a2 is None:
        in_specs = [pl.BlockSpec((tm, tk), lambda i, j, k: (i, k)), w_spec]
    else:
        assert tk == kdim + a2.shape[1]
        in_specs = [pl.BlockSpec((tm, kdim), lambda i, j, k: (i, 0)), w_spec,
                    pl.BlockSpec((tm, a2.shape[1]), lambda i, j, k: (i, 0))]
        args.append(a2)
        kdim = tk
    nk = kdim // tk
    emit_norm = next_gain is not None
    if ssq is not None:
        in_specs.append(pl.BlockSpec((tm, LANES), lambda i, j, k: (i, 0)))
        args.append(ssq)
    if epilogue == "residual":
        in_specs.append(pl.BlockSpec((tm, tn), lambda i, j, k: (i, j)))
        args.append(residual)
        if emit_norm:
            in_specs.append(pl.BlockSpec((1, tn), lambda i, j, k: (0, j)))
            args.append(next_gain.reshape(1, n))
    out_specs = [pl.BlockSpec((tm, tn), lambda i, j, k: (i, j))]
    out_shape = [jax.ShapeDtypeStruct((m, n), out_dtype)]
    if emit_norm:
        out_specs += [pl.BlockSpec((tm, tn), lambda i, j, k: (i, j)),
                      pl.BlockSpec((tm, LANES), lambda i, j, k: (i, 0))]
        out_shape += [jax.ShapeDtypeStruct((m, n), BF16), jax.ShapeDtypeStruct((m, LANES), F32)]
    scratch = [pltpu.VMEM((tm, tn), F32)] if nk > 1 else []
    out = pl.pallas_call(
        functools.partial(_mm_kernel, nk=nk, kdim=kdim, two_a=a2 is not None, rowscale=ssq is not None,
                          epilogue=epilogue, emit_norm=emit_norm),
        grid=(m // tm, n // tn, nk),
        in_specs=in_specs,
        out_specs=out_specs,
        out_shape=out_shape,
        scratch_shapes=scratch,
        compiler_params=_params(("parallel", "arbitrary", "arbitrary")),
        name="matmul_" + (epilogue or "plain"),
    )(*args)
    return out if emit_norm else out[0]


CONV_TS = 256
CONV_HALO = 32
CONV_CW = 128


def _conv_kernel(a_ref, b_ref, w_ref, cb_ref, g_ref, bb_ref, o_ref, zbuf, ybuf):
    i = pl.program_id(0)
    ts = CONV_TS

    @pl.when(i == 0)
    def _():
        zbuf[0:CONV_HALO, :] = jnp.zeros((CONV_HALO, CONV_CH), F32)
        zbuf[CONV_HALO + ts:, :] = jnp.zeros((SUBLANES, CONV_CH), F32)

    @pl.when(i > 0)
    def _():
        zbuf[0:CONV_HALO, :] = zbuf[ts:ts + CONV_HALO, :]

    zbuf[CONV_HALO:CONV_HALO + ts, :] = a_ref[...] * jax.nn.sigmoid(b_ref[...])

    base = CONV_HALO - (CONV_K - 1)

    def chunk(c, carry):
        cols = pl.ds(pl.multiple_of(c * CONV_CW, CONV_CW), CONV_CW)
        acc = jnp.zeros((ts, CONV_CW), F32)
        for phase in range(SUBLANES):
            part = jnp.zeros((ts + SUBLANES, CONV_CW), F32)
            for k in range(CONV_K):
                if (base + k) % SUBLANES == phase:
                    off = base + k - phase
                    part = part + w_ref[k:k + 1, cols] * zbuf[off:off + ts + SUBLANES, cols]
            acc = acc + part[phase:phase + ts, :]
        ybuf[:, cols] = acc + cb_ref[:, cols]
        return carry

    lax.fori_loop(0, CONV_CH // CONV_CW, chunk, 0)

    y = ybuf[...]
    mu = jnp.mean(y, axis=-1, keepdims=True)
    yc = y - mu
    var = jnp.mean(yc * yc, axis=-1, keepdims=True)
    yn = yc * lax.rsqrt(var + EPS) * g_ref[...] + bb_ref[...]
    o_ref[...] = (yn * jax.nn.sigmoid(yn)).astype(o_ref.dtype)


def _conformer_conv(h, conv_w, conv_b, ln_g, ln_b):
    s = h.shape[0]
    c = CONV_CH
    wpad = jnp.zeros((CONV_HALO, c), F32).at[:CONV_K].set(conv_w)
    row = lambda v: v.reshape(1, c)
    full = lambda r: pl.BlockSpec((r, c), lambda i: (0, 0))
    return pl.pallas_call(
        _conv_kernel,
        grid=(s // CONV_TS,),
        in_specs=[pl.BlockSpec((CONV_TS, c), lambda i: (i, 0)),
                  pl.BlockSpec((CONV_TS, c), lambda i: (i, 1)),
                  full(CONV_HALO), full(1), full(1), full(1)],
        out_specs=pl.BlockSpec((CONV_TS, c), lambda i: (i, 0)),
        out_shape=jax.ShapeDtypeStruct((s, MIX_W), BF16),
        scratch_shapes=[pltpu.VMEM((CONV_TS + CONV_HALO + SUBLANES, c), F32),
                        pltpu.VMEM((CONV_TS, c), F32)],
        compiler_params=_params(("arbitrary",)),
        name="conformer_conv",
    )(h, h, wpad, row(conv_b), row(ln_g), row(ln_b))


XATTN_TS = 512


def _head_rmsnorm(x, gain, scale=1.0):
    ms = jnp.mean(x * x, axis=-1, keepdims=True)
    return x * (lax.rsqrt(ms + EPS) * scale) * gain


def _xattn_kernel(q_ref, kv_ref, qg_ref, kg_ref, o_ref):
    dh = X_HEAD_DIM
    for h in range(X_HEADS):
        q = _head_rmsnorm(q_ref[:, h * dh:(h + 1) * dh], qg_ref[...], dh ** -0.5 * LOG2E).astype(BF16)
        k = _head_rmsnorm(kv_ref[:, h * dh:(h + 1) * dh], kg_ref[...]).astype(BF16)
        v = kv_ref[:, X_W + h * dh:X_W + (h + 1) * dh].astype(BF16)
        s = lax.dot_general(q, k, _NT, preferred_element_type=F32)
        p = jnp.exp2(s - jnp.max(s, axis=-1, keepdims=True))
        l = jnp.sum(p, axis=-1, keepdims=True)
        o = jnp.dot(p.astype(BF16), v, preferred_element_type=F32) / l
        o_ref[:, h * dh:(h + 1) * dh] = o.astype(o_ref.dtype)


def _memory_xattn(h, qx_off, kv, q_gain, k_gain):
    s = h.shape[0]
    n_mem = kv.shape[0]
    qblk = qx_off // X_W
    return pl.pallas_call(
        _xattn_kernel,
        grid=(s // XATTN_TS,),
        in_specs=[pl.BlockSpec((XATTN_TS, X_W), lambda i: (i, qblk)),
                  pl.BlockSpec((n_mem, 2 * X_W), lambda i: (0, 0)),
                  pl.BlockSpec((1, X_HEAD_DIM), lambda i: (0, 0)),
                  pl.BlockSpec((1, X_HEAD_DIM), lambda i: (0, 0))],
        out_specs=pl.BlockSpec((XATTN_TS, X_W), lambda i: (i, 0)),
        out_shape=jax.ShapeDtypeStruct((s, X_W), BF16),
        compiler_params=_params(("parallel",)),
        name="memory_xattn",
    )(h, kv, q_gain.reshape(1, -1), k_gain.reshape(1, -1))


KVPREP_TS = 512


def _kvprep_kernel(x_ref, g_ref, o_ref):
    dh = NSA_HEAD_DIM
    for part in range(4):
        for g in range(NSA_KV_HEADS):
            lo = part * NSA_KVW + g * dh
            x = x_ref[:, lo:lo + dh]
            if part % 2 == 0:
                x = _head_rmsnorm(x, g_ref[part // 2:part // 2 + 1, :])
            o_ref[:, lo:lo + dh] = x.astype(o_ref.dtype)


def _nsa_kvprep(h, k_gain):
    s = h.shape[0]
    w = 4 * NSA_KVW
    blk = (NSA_QW + 2 * NSA_KVW) // w
    return pl.pallas_call(
        _kvprep_kernel,
        grid=(s // KVPREP_TS,),
        in_specs=[pl.BlockSpec((KVPREP_TS, w), lambda i: (i, blk)),
                  pl.BlockSpec((2, NSA_HEAD_DIM), lambda i: (0, 0))],
        out_specs=pl.BlockSpec((KVPREP_TS, w), lambda i: (i, 0)),
        out_shape=jax.ShapeDtypeStruct((s, w), BF16),
        compiler_params=_params(("parallel",)),
        name="nsa_kvprep",
    )(h, k_gain[1:3])


def _compress_kernel(c_ref, pa_ref, pb_ref, w1_ref, w2_ref, kg_ref, o_ref):
    half = CMP_STRIDE * NSA_HEAD_DIM
    c = c_ref[...]
    first = jnp.dot((c + pa_ref[...]).astype(BF16), w1_ref[0:half, :], preferred_element_type=F32)
    second = jnp.dot((c + pb_ref[...]).astype(BF16), w1_ref[half:2 * half, :], preferred_element_type=F32)
    nchunk = c.shape[0]
    hid = first + pltpu.roll(second, nchunk - 1, 0)
    hid = hid * jax.nn.sigmoid(hid)
    out = jnp.dot(hid.astype(BF16), w2_ref[...], preferred_element_type=F32)
    normed = _head_rmsnorm(out, kg_ref[...])
    is_key = pl.program_id(0) == 0
    o_ref[...] = jnp.where(is_key, normed, out).astype(o_ref.dtype)


def _nsa_compress(h, cmp_pos, cmp_w1, cmp_w2, k_gain):
    s = h.shape[0]
    g, dh = NSA_KV_HEADS, NSA_HEAD_DIM
    nchunk = s // CMP_STRIDE
    half = CMP_STRIDE * dh
    c = h[:, NSA_QW:NSA_QW + 2 * NSA_KVW].reshape(s, 2, g, dh).transpose(1, 2, 0, 3).reshape(2, g, nchunk, half)
    pos = cmp_pos.reshape(2, 2, 1, half)
    return pl.pallas_call(
        _compress_kernel,
        grid=(2, g),
        in_specs=[pl.BlockSpec((None, None, nchunk, half), lambda a, b: (a, b, 0, 0)),
                  pl.BlockSpec((None, None, 1, half), lambda a, b: (a, 0, 0, 0)),
                  pl.BlockSpec((None, None, 1, half), lambda a, b: (a, 1, 0, 0)),
                  pl.BlockSpec((None, 2 * half, CMP_HIDDEN), lambda a, b: (a, 0, 0)),
                  pl.BlockSpec((None, CMP_HIDDEN, dh), lambda a, b: (a, 0, 0)),
                  pl.BlockSpec((1, dh), lambda a, b: (0, 0))],
        out_specs=pl.BlockSpec((None, None, nchunk, dh), lambda a, b: (a, b, 0, 0)),
        out_shape=jax.ShapeDtypeStruct((2, g, nchunk, dh), BF16),
        compiler_params=_params(("parallel", "parallel")),
        name="nsa_compress",
    )(c, pos, pos, cmp_w1.astype(BF16), cmp_w2.astype(BF16), k_gain[0:1])


NSA_TQ = 128
NSA_TK = 512
WIN_KEYS = WINDOW + NSA_TQ


def _softmax_parts(s3, bias):
    s3 = s3 + bias[None]
    m = jnp.maximum(jnp.max(s3, axis=-1, keepdims=True), M_FLOOR)
    p = jnp.exp2(s3 - m)
    return p, jnp.sum(p, axis=-1, keepdims=True)


def _safe_inv(l):
    return jnp.where(l > 0.0, 1.0 / l, 0.0)


def _top_n_membership(score_t):
    nblk = score_t.shape[0]
    j_col = lax.broadcasted_iota(jnp.int32, (nblk, 1), 0).astype(F32)
    sel = jnp.zeros(score_t.shape, F32)
    s = score_t
    for _ in range(N_SEL):
        top = jnp.max(s, axis=0, keepdims=True)
        first = jnp.min(jnp.where(s == top, j_col, float(nblk)), axis=0, keepdims=True)
        hit = j_col == first
        sel = jnp.where(hit, 1.0, sel)
        s = jnp.where(hit, -jnp.inf, s)
    return sel


def _nsa_kernel(q_ref, gate_ref, kc_ref, vc_ref, ks_ref, vs_ref, kw_ref, vw_ref, qg_ref, o_ref):
    r_, tq, dh, tk = NSA_GROUP, NSA_TQ, NSA_HEAD_DIM, NSA_TK
    i = pl.program_id(1)
    start = i * tq
    ncmp = kc_ref.shape[0]
    nslc = LANES
    t_col = start + lax.broadcasted_iota(jnp.int32, (tq, 1), 0)

    q = jnp.concatenate(
        [_head_rmsnorm(q_ref[:, r * dh:(r + 1) * dh], qg_ref[...], dh ** -0.5 * LOG2E).astype(BF16)
         for r in range(r_)], axis=0)

    n_row = lax.broadcasted_iota(jnp.int32, (1, ncmp), 1)
    bias_c = jnp.where(n_row * CMP_STRIDE + (CMP_LEN - 1) <= t_col, 0.0, NEG)
    s = lax.dot_general(q, kc_ref[...], _NT, preferred_element_type=F32)
    p, l = _softmax_parts(s.reshape(r_, tq, ncmp), bias_c)
    p = p * _safe_inv(l)
    o_c = jnp.dot(p.reshape(r_ * tq, ncmp).astype(BF16), vc_ref[...], preferred_element_type=F32)

    psum = jnp.sum(p, axis=0)
    n_col = lax.broadcasted_iota(jnp.int32, (ncmp, 1), 0) * CMP_STRIDE
    j_row = lax.broadcasted_iota(jnp.int32, (1, nslc), 1)
    overlap = jnp.where((n_col < (j_row + 1) * SEL_BLOCK) & (n_col + CMP_LEN > j_row * SEL_BLOCK),
                        1.0, 0.0).astype(BF16)
    p_hi = psum.astype(BF16)
    p_lo = (psum - p_hi.astype(F32)).astype(BF16)
    imp = (jnp.dot(p_hi, overlap, preferred_element_type=F32)
           + jnp.dot(p_lo, overlap, preferred_element_type=F32))
    cur = t_col >> 6
    valid = j_row * SEL_BLOCK <= t_col
    forced = (j_row == 0) | (j_row == cur) | (j_row == cur - 1)
    score = jnp.where(valid, imp + jnp.where(forced, FORCE, 0.0), NEG)
    sel = _top_n_membership(score.T).T.astype(BF16)
    j_col = lax.broadcasted_iota(jnp.int32, (nslc, 1), 0)

    def sel_body(c, carry):
        m, l, acc = carry
        rows = pl.ds(pl.multiple_of(c * tk, tk), tk)
        s = lax.dot_general(q, ks_ref[rows, :], _NT, preferred_element_type=F32).reshape(r_, tq, tk)
        kl = lax.broadcasted_iota(jnp.int32, (1, tk), 1)
        expand = jnp.where(j_col == c * (tk // SEL_BLOCK) + (kl >> 6), 1.0, 0.0).astype(BF16)
        selx = jnp.dot(sel, expand, preferred_element_type=F32)
        bias = jnp.where((selx > 0.5) & (c * tk + kl <= t_col), 0.0, NEG)
        s = s + bias[None]
        m_new = jnp.maximum(m, jnp.max(s, axis=-1, keepdims=True))
        alpha = jnp.exp2(m - m_new)
        p = jnp.exp2(s - m_new)
        l = alpha * l + jnp.sum(p, axis=-1, keepdims=True)
        pv = jnp.dot(p.reshape(r_ * tq, tk).astype(BF16), vs_ref[rows, :], preferred_element_type=F32)
        acc = alpha * acc + pv.reshape(r_, tq, dh)
        return m_new, l, acc

    n_tiles = (start + tq + tk - 1) // tk
    init = (jnp.full((r_, tq, 1), M_FLOOR, F32), jnp.zeros((r_, tq, 1), F32), jnp.zeros((r_, tq, dh), F32))
    _, l_s, acc_s = lax.fori_loop(0, n_tiles, sel_body, init)
    o_s = acc_s * _safe_inv(l_s)

    lo = pl.multiple_of(jnp.maximum(start - WINDOW, 0), tq)
    kpos = lo + lax.broadcasted_iota(jnp.int32, (1, WIN_KEYS), 1)
    bias_w = jnp.where((kpos <= t_col) & (kpos > t_col - WINDOW), 0.0, NEG)
    s = lax.dot_general(q, kw_ref[pl.ds(lo, WIN_KEYS), :], _NT, preferred_element_type=F32)
    p, l = _softmax_parts(s.reshape(r_, tq, WIN_KEYS), bias_w)
    o_w = jnp.dot(p.reshape(r_ * tq, WIN_KEYS).astype(BF16), vw_ref[pl.ds(lo, WIN_KEYS), :],
                  preferred_element_type=F32).reshape(r_, tq, dh) * _safe_inv(l)

    gates = jax.nn.sigmoid(gate_ref[...])
    o_c = o_c.reshape(r_, tq, dh)
    for r in range(r_):
        gate = lambda b: gates[:, b * GATE_STRIDE + r:b * GATE_STRIDE + r + 1]
        o = gate(0) * o_c[r] + gate(1) * o_s[r] + gate(2) * o_w[r]
        o_ref[:, r * dh:(r + 1) * dh] = o.astype(o_ref.dtype)


def _nsa_attention(h, h_tail, cmp_kv, kvn, q_gain):
    s = h.shape[0]
    g, dh = NSA_KV_HEADS, NSA_HEAD_DIM
    qw = NSA_GROUP * dh
    ncmp = cmp_kv.shape[2]
    gate_blk = X_W // LANES
    kv_spec = lambda part: pl.BlockSpec((s, dh), lambda a, i: (0, part * g + a))
    return pl.pallas_call(
        _nsa_kernel,
        grid=(g, s // NSA_TQ),
        in_specs=[pl.BlockSpec((NSA_TQ, qw), lambda a, i: (i, a)),
                  pl.BlockSpec((NSA_TQ, LANES), lambda a, i: (i, gate_blk + a)),
                  pl.BlockSpec((None, None, ncmp, dh), lambda a, i: (0, a, 0, 0)),
                  pl.BlockSpec((None, None, ncmp, dh), lambda a, i: (1, a, 0, 0)),
                  kv_spec(0), kv_spec(1), kv_spec(2), kv_spec(3),
                  pl.BlockSpec((1, dh), lambda a, i: (0, 0))],
        out_specs=pl.BlockSpec((NSA_TQ, qw), lambda a, i: (i, a)),
        out_shape=jax.ShapeDtypeStruct((s, MIX_W), BF16),
        compiler_params=_params(("parallel", "arbitrary")),
        name="nsa_attention",
    )(h, h_tail, cmp_kv, cmp_kv, kvn, kvn, kvn, kvn, q_gain.reshape(1, dh))


def _nsa_tail_weight(w):
    gate_cols = jnp.zeros((w.shape[0], NSA_KV_HEADS, LANES), w.dtype)
    gsrc = w[:, NSA_MAIN_COLS:NSA_MAIN_COLS + NSA_GW].reshape(w.shape[0], NSA_KV_HEADS, NSA_GROUP, 3)
    for b in range(3):
        gate_cols = gate_cols.at[:, :, b * GATE_STRIDE:b * GATE_STRIDE + NSA_GROUP].set(gsrc[..., b])
    return jnp.concatenate([w[:, NSA_COLS:], gate_cols.reshape(w.shape[0], -1)], axis=1)


MM_TM = 1024
NORM_TM = 256


def kernel(x, mem, attn_norm, mem_norm, w_mem_kv, xq_gain, xk_gain, w_out, mlp_norm, w_up, w_down,
           cf_w_in, cf_conv_w, cf_conv_b, cf_ln_g, cf_ln_b,
           nsa_w_in, nsa_q_gain, nsa_k_gain, nsa_cmp_pos, nsa_cmp_w1, nsa_cmp_w2):
    b, s, d = x.shape
    assert b == 1 and d == D_MODEL and s % MM_TM == 0 and s // SEL_BLOCK == LANES
    x2 = x.reshape(s, d)
    mem2 = mem.reshape(mem.shape[1], d)
    n_mem = mem2.shape[0]
    mm = functools.partial(_matmul, tm=MM_TM, tk=d)
    down_w = w_down.astype(BF16)
    xg, ssq = _prenorm(x2, attn_norm[0], NORM_TM)
    for i in range(DEPTH):
        j = i // 2
        if i % 2 == 0:
            h = mm(xg, cf_w_in, layer=j, tn=512, out_dtype=F32, ssq=ssq)
            mix = _conformer_conv(h, cf_conv_w[j], cf_conv_b[j], cf_ln_g[j], cf_ln_b[j])
            h_x, qx_off = h, CF_QX_OFF
        else:
            h = mm(xg, nsa_w_in, layer=j, tn=512, n=NSA_MAIN_COLS, out_dtype=F32, ssq=ssq)
            h_x = mm(xg, _nsa_tail_weight(nsa_w_in[j]), tn=512, out_dtype=F32, ssq=ssq)
            cmp_kv = _nsa_compress(h, nsa_cmp_pos[j], nsa_cmp_w1[j], nsa_cmp_w2[j], nsa_k_gain[j])
            kvn = _nsa_kvprep(h, nsa_k_gain[j])
            mix = _nsa_attention(h, h_x, cmp_kv, kvn, nsa_q_gain[j])
            qx_off = 0
        memg, mem_ssq = _prenorm(mem2, mem_norm[i], n_mem)
        kv = _matmul(memg, w_mem_kv, layer=i, tm=n_mem, tn=1024, tk=d, out_dtype=F32, ssq=mem_ssq)
        xo = _memory_xattn(h_x, qx_off, kv, xq_gain[i], xk_gain[i])
        x2, xg, ssq = mm(mix, w_out, layer=i, a2=xo, tn=512, out_dtype=F32,
                         epilogue="residual", residual=x2, next_gain=mlp_norm[i])
        u = mm(xg, w_up, layer=i, tn=512, out_dtype=BF16, ssq=ssq, epilogue="relu2")
        down = functools.partial(_matmul, u, down_w, layer=i, tm=512, tn=256, tk=D_FF, out_dtype=F32,
                                 epilogue="residual", residual=x2)
        if i + 1 < DEPTH:
            x2, xg, ssq = down(next_gain=attn_norm[i + 1])
        else:
            x2 = down()
    return x2.reshape(b, s, d)
```

```python
import functools
import math

import jax
import jax.numpy as jnp
from jax import lax
from jax.experimental import pallas as pl
from jax.experimental.pallas import tpu as pltpu

D_MODEL = 4096
DEPTH = 4
MIX_W = 3 * D_MODEL // 4
X_W = D_MODEL // 4
X_HEADS = 4
X_HEAD_DIM = X_W // X_HEADS
CONV_CH = MIX_W
CONV_K = 31
NSA_HEAD_DIM = 128
NSA_HEADS = MIX_W // NSA_HEAD_DIM
NSA_KV_HEADS = 4
NSA_GROUP = NSA_HEADS // NSA_KV_HEADS
CMP_STRIDE = 16
CMP_LEN = 2 * CMP_STRIDE
CMP_HIDDEN = 256
SEL_BLOCK = 64
N_SEL = 16
WINDOW = 512
D_FF = 4 * D_MODEL
NSA_QW = NSA_HEADS * NSA_HEAD_DIM
NSA_KVW = NSA_KV_HEADS * NSA_HEAD_DIM
NSA_GW = NSA_HEADS * 3
NSA_COLS = NSA_QW + 6 * NSA_KVW + NSA_GW
EPS = 1e-6
NEG = -1e30
FORCE = 1e4

LANES = 128
SUBLANES = 8
VMEM_LIMIT = 56 * 1024 * 1024

M_FLOOR = -1e20
LOG2E = math.log2(math.e)

NSA_MAIN_COLS = NSA_QW + 6 * NSA_KVW
NSA_TAIL_COLS = X_W + NSA_KV_HEADS * LANES
GATE_STRIDE = 8
CF_QX_OFF = 2 * CONV_CH

F32 = jnp.float32
BF16 = jnp.bfloat16
_NT = (((1,), (1,)), ((), ()))


def _params(sem):
    return pltpu.CompilerParams(dimension_semantics=sem, vmem_limit_bytes=VMEM_LIMIT)


def _prenorm_kernel(x_ref, g_ref, xg_ref, ssq_ref):
    x = x_ref[...]
    xg_ref[...] = (x * g_ref[...]).astype(xg_ref.dtype)
    ssq_ref[...] = jnp.broadcast_to(jnp.sum(x * x, axis=-1, keepdims=True), ssq_ref.shape)


def _prenorm(x, g, tm):
    m, d = x.shape
    return pl.pallas_call(
        _prenorm_kernel,
        grid=(m // tm,),
        in_specs=[pl.BlockSpec((tm, d), lambda i: (i, 0)),
                  pl.BlockSpec((1, d), lambda i: (0, 0))],
        out_specs=[pl.BlockSpec((tm, d), lambda i: (i, 0)),
                   pl.BlockSpec((tm, LANES), lambda i: (i, 0))],
        out_shape=[jax.ShapeDtypeStruct((m, d), BF16), jax.ShapeDtypeStruct((m, LANES), F32)],
        compiler_params=_params(("parallel",)),
        name="prenorm",
    )(x, g.reshape(1, d))


def _mm_kernel(*refs, nk, kdim, two_a, rowscale, epilogue, emit_norm):
    refs = list(refs)
    a_ref, w_ref = refs[:2]
    pos = 2
    a2_ref = ssq_in = r_ref = g_ref = xg_ref = ssq_out = None
    if two_a:
        a2_ref = refs[pos]
        pos += 1
    if rowscale:
        ssq_in = refs[pos]
        pos += 1
    if epilogue == "residual":
        r_ref = refs[pos]
        pos += 1
        if emit_norm:
            g_ref = refs[pos]
            pos += 1
    o_ref = refs[pos]
    pos += 1
    if emit_norm:
        xg_ref, ssq_out = refs[pos:pos + 2]
        pos += 2
    scratch = refs[pos:]

    def finish(acc):
        if rowscale:
            acc = acc * lax.rsqrt(ssq_in[:, 0:1] * (1.0 / kdim) + EPS)
        if epilogue == "residual":
            acc = acc + r_ref[...]
        elif epilogue == "relu2":
            acc = jnp.square(jnp.maximum(acc, 0.0))
        o_ref[...] = acc.astype(o_ref.dtype)
        if emit_norm:
            xg_ref[...] = (acc * g_ref[...]).astype(xg_ref.dtype)
            part = jnp.broadcast_to(jnp.sum(acc * acc, axis=-1, keepdims=True), ssq_out.shape)
            j = pl.program_id(1)

            @pl.when(j == 0)
            def _():
                ssq_out[...] = part

            @pl.when(j > 0)
            def _():
                ssq_out[...] += part

    if two_a:
        k1 = a_ref.shape[1]
        part = (jnp.dot(a_ref[...], w_ref[0:k1, :].astype(BF16), preferred_element_type=F32)
                + jnp.dot(a2_ref[...], w_ref[k1:, :].astype(BF16), preferred_element_type=F32))
    else:
        part = jnp.dot(a_ref[...], w_ref[...].astype(BF16), preferred_element_type=F32)
    if nk == 1:
        finish(part)
        return
    acc_ref, = scratch
    k = pl.program_id(2)

    @pl.when(k == 0)
    def _():
        acc_ref[...] = part

    @pl.when(jnp.logical_and(k > 0, k < nk - 1))
    def _():
        acc_ref[...] += part

    @pl.when(k == nk - 1)
    def _():
        finish(acc_ref[...] + part)


def _matmul(a, w, *, tm, tn, tk, out_dtype, layer=None, n=None, a2=None, ssq=None, epilogue=None,
            residual=None, next_gain=None):
    m, kdim = a.shape
    n = w.shape[-1] if n is None else n
    if layer is None:
        w_spec = pl.BlockSpec((tk, tn), lambda i, j, k: (k, j))
    else:
        w_spec = pl.BlockSpec((None, tk, tn), lambda i, j, k: (layer, k, j))
    args = [a, w]
    if a2 is None:
        in_specs = [pl.BlockSpec((tm, tk), lambda i, j, k: (i, k)), w_spec]
    else:
        assert tk == kdim + a2.shape[1]
        in_specs = [pl.BlockSpec((tm, kdim), lambda i, j, k: (i, 0)), w_spec,
                    pl.BlockSpec((tm, a2.shape[1]), lambda i, j, k: (i, 0))]
        args.append(a2)
        kdim = tk
    nk = kdim // tk
    emit_norm = next_gain is not None
    if ssq is not None:
        in_specs.append(pl.BlockSpec((tm, LANES), lambda i, j, k: (i, 0)))
        args.append(ssq)
    if epilogue == "residual":
        in_specs.append(pl.BlockSpec((tm, tn), lambda i, j, k: (i, j)))
        args.append(residual)
        if emit_norm:
            in_specs.append(pl.BlockSpec((1, tn), lambda i, j, k: (0, j)))
            args.append(next_gain.reshape(1, n))
    out_specs = [pl.BlockSpec((tm, tn), lambda i, j, k: (i, j))]
    out_shape = [jax.ShapeDtypeStruct((m, n), out_dtype)]
    if emit_norm:
        out_specs += [pl.BlockSpec((tm, tn), lambda i, j, k: (i, j)),
                      pl.BlockSpec((tm, LANES), lambda i, j, k: (i, 0))]
        out_shape += [jax.ShapeDtypeStruct((m, n), BF16), jax.ShapeDtypeStruct((m, LANES), F32)]
    scratch = [pltpu.VMEM((tm, tn), F32)] if nk > 1 else []
    out = pl.pallas_call(
        functools.partial(_mm_kernel, nk=nk, kdim=kdim, two_a=a2 is not None, rowscale=ssq is not None,
                          epilogue=epilogue, emit_norm=emit_norm),
        grid=(m // tm, n // tn, nk),
        in_specs=in_specs,
        out_specs=out_specs,
        out_shape=out_shape,
        scratch_shapes=scratch,
        compiler_params=_params(("parallel", "arbitrary", "arbitrary")),
        name="matmul_" + (epilogue or "plain"),
    )(*args)
    return out if emit_norm else out[0]


CONV_TS = 256
CONV_HALO = 32
CONV_CW = 128


def _conv_kernel(a_ref, b_ref, w_ref, cb_ref, g_ref, bb_ref, o_ref, zbuf, ybuf):
    i = pl.program_id(0)
    ts = CONV_TS

    @pl.when(i == 0)
    def _():
        zbuf[0:CONV_HALO, :] = jnp.zeros((CONV_HALO, CONV_CH), F32)
        zbuf[CONV_HALO + ts:, :] = jnp.zeros((SUBLANES, CONV_CH), F32)

    @pl.when(i > 0)
    def _():
        zbuf[0:CONV_HALO, :] = zbuf[ts:ts + CONV_HALO, :]

    zbuf[CONV_HALO:CONV_HALO + ts, :] = a_ref[...] * jax.nn.sigmoid(b_ref[...])

    base = CONV_HALO - (CONV_K - 1)

    def chunk(c, carry):
        cols = pl.ds(pl.multiple_of(c * CONV_CW, CONV_CW), CONV_CW)
        acc = jnp.zeros((ts, CONV_CW), F32)
        for phase in range(SUBLANES):
            part = jnp.zeros((ts + SUBLANES, CONV_CW), F32)
            for k in range(CONV_K):
                if (base + k) % SUBLANES == phase:
                    off = base + k - phase
                    part = part + w_ref[k:k + 1, cols] * zbuf[off:off + ts + SUBLANES, cols]
            acc = acc + part[phase:phase + ts, :]
        ybuf[:, cols] = acc + cb_ref[:, cols]
        return carry

    lax.fori_loop(0, CONV_CH // CONV_CW, chunk, 0)

    y = ybuf[...]
    mu = jnp.mean(y, axis=-1, keepdims=True)
    yc = y - mu
    var = jnp.mean(yc * yc, axis=-1, keepdims=True)
    yn = yc * lax.rsqrt(var + EPS) * g_ref[...] + bb_ref[...]
    o_ref[...] = (yn * jax.nn.sigmoid(yn)).astype(o_ref.dtype)


def _conformer_conv(h, conv_w, conv_b, ln_g, ln_b):
    s = h.shape[0]
    c = CONV_CH
    wpad = jnp.zeros((CONV_HALO, c), F32).at[:CONV_K].set(conv_w)
    row = lambda v: v.reshape(1, c)
    full = lambda r: pl.BlockSpec((r, c), lambda i: (0, 0))
    return pl.pallas_call(
        _conv_kernel,
        grid=(s // CONV_TS,),
        in_specs=[pl.BlockSpec((CONV_TS, c), lambda i: (i, 0)),
                  pl.BlockSpec((CONV_TS, c), lambda i: (i, 1)),
                  full(CONV_HALO), full(1), full(1), full(1)],
        out_specs=pl.BlockSpec((CONV_TS, c), lambda i: (i, 0)),
        out_shape=jax.ShapeDtypeStruct((s, MIX_W), BF16),
        scratch_shapes=[pltpu.VMEM((CONV_TS + CONV_HALO + SUBLANES, c), F32),
                        pltpu.VMEM((CONV_TS, c), F32)],
        compiler_params=_params(("arbitrary",)),
        name="conformer_conv",
    )(h, h, wpad, row(conv_b), row(ln_g), row(ln_b))


XATTN_TS = 512


def _head_rmsnorm(x, gain, scale=1.0):
    ms = jnp.mean(x * x, axis=-1, keepdims=True)
    return x * (lax.rsqrt(ms + EPS) * scale) * gain


def _xattn_kernel(q_ref, kv_ref, qg_ref, kg_ref, o_ref):
    dh = X_HEAD_DIM
    for h in range(X_HEADS):
        q = _head_rmsnorm(q_ref[:, h * dh:(h + 1) * dh], qg_ref[...], dh ** -0.5 * LOG2E).astype(BF16)
        k = _head_rmsnorm(kv_ref[:, h * dh:(h + 1) * dh], kg_ref[...]).astype(BF16)
        v = kv_ref[:, X_W + h * dh:X_W + (h + 1) * dh].astype(BF16)
        s = lax.dot_general(q, k, _NT, preferred_element_type=F32)
        p = jnp.exp2(s - jnp.max(s, axis=-1, keepdims=True))
        l = jnp.sum(p, axis=-1, keepdims=True)
        o = jnp.dot(p.astype(BF16), v, preferred_element_type=F32) / l
        o_ref[:, h * dh:(h + 1) * dh] = o.astype(o_ref.dtype)


def _memory_xattn(h, qx_off, kv, q_gain, k_gain):
    s = h.shape[0]
    n_mem = kv.shape[0]
    qblk = qx_off // X_W
    return pl.pallas_call(
        _xattn_kernel,
        grid=(s // XATTN_TS,),
        in_specs=[pl.BlockSpec((XATTN_TS, X_W), lambda i: (i, qblk)),
                  pl.BlockSpec((n_mem, 2 * X_W), lambda i: (0, 0)),
                  pl.BlockSpec((1, X_HEAD_DIM), lambda i: (0, 0)),
                  pl.BlockSpec((1, X_HEAD_DIM), lambda i: (0, 0))],
        out_specs=pl.BlockSpec((XATTN_TS, X_W), lambda i: (i, 0)),
        out_shape=jax.ShapeDtypeStruct((s, X_W), BF16),
        compiler_params=_params(("parallel",)),
        name="memory_xattn",
    )(h, kv, q_gain.reshape(1, -1), k_gain.reshape(1, -1))


KVPREP_TS = 512


def _kvprep_kernel(x_ref, g_ref, o_ref):
    dh = NSA_HEAD_DIM
    for part in range(4):
        for g in range(NSA_KV_HEADS):
            lo = part * NSA_KVW + g * dh
            x = x_ref[:, lo:lo + dh]
            if part % 2 == 0:
                x = _head_rmsnorm(x, g_ref[part // 2:part // 2 + 1, :])
            o_ref[:, lo:lo + dh] = x.astype(o_ref.dtype)


def _nsa_kvprep(h, k_gain):
    s = h.shape[0]
    w = 4 * NSA_KVW
    blk = (NSA_QW + 2 * NSA_KVW) // w
    return pl.pallas_call(
        _kvprep_kernel,
        grid=(s // KVPREP_TS,),
        in_specs=[pl.BlockSpec((KVPREP_TS, w), lambda i: (i, blk)),
                  pl.BlockSpec((2, NSA_HEAD_DIM), lambda i: (0, 0))],
        out_specs=pl.BlockSpec((KVPREP_TS, w), lambda i: (i, 0)),
        out_shape=jax.ShapeDtypeStruct((s, w), BF16),
        compiler_params=_params(("parallel",)),
        name="nsa_kvprep",
    )(h, k_gain[1:3])


def _compress_kernel(c_ref, pa_ref, pb_ref, w1_ref, w2_ref, kg_ref, o_ref):
    half = CMP_STRIDE * NSA_HEAD_DIM
    c = c_ref[...]
    first = jnp.dot((c + pa_ref[...]).astype(BF16), w1_ref[0:half, :], preferred_element_type=F32)
    second = jnp.dot((c + pb_ref[...]).astype(BF16), w1_ref[half:2 * half, :], preferred_element_type=F32)
    nchunk = c.shape[0]
    hid = first + pltpu.roll(second, nchunk - 1, 0)
    hid = hid * jax.nn.sigmoid(hid)
    out = jnp.dot(hid.astype(BF16), w2_ref[...], preferred_element_type=F32)
    normed = _head_rmsnorm(out, kg_ref[...])
    is_key = pl.program_id(0) == 0
    o_ref[...] = jnp.where(is_key, normed, out).astype(o_ref.dtype)


def _nsa_compress(h, cmp_pos, cmp_w1, cmp_w2, k_gain):
    s = h.shape[0]
    g, dh = NSA_KV_HEADS, NSA_HEAD_DIM
    nchunk = s // CMP_STRIDE
    half = CMP_STRIDE * dh
    c = h[:, NSA_QW:NSA_QW + 2 * NSA_KVW].reshape(s, 2, g, dh).transpose(1, 2, 0, 3).reshape(2, g, nchunk, half)
    pos = cmp_pos.reshape(2, 2, 1, half)
    return pl.pallas_call(
        _compress_kernel,
        grid=(2, g),
        in_specs=[pl.BlockSpec((None, None, nchunk, half), lambda a, b: (a, b, 0, 0)),
                  pl.BlockSpec((None, None, 1, half), lambda a, b: (a, 0, 0, 0)),
                  pl.BlockSpec((None, None, 1, half), lambda a, b: (a, 1, 0, 0)),
                  pl.BlockSpec((None, 2 * half, CMP_HIDDEN), lambda a, b: (a, 0, 0)),
                  pl.BlockSpec((None, CMP_HIDDEN, dh), lambda a, b: (a, 0, 0)),
                  pl.BlockSpec((1, dh), lambda a, b: (0, 0))],
        out_specs=pl.BlockSpec((None, None, nchunk, dh), lambda a, b: (a, b, 0, 0)),
        out_shape=jax.ShapeDtypeStruct((2, g, nchunk, dh), BF16),
        compiler_params=_params(("parallel", "parallel")),
        name="nsa_compress",
    )(c, pos, pos, cmp_w1.astype(BF16), cmp_w2.astype(BF16), k_gain[0:1])


NSA_TQ = 128
NSA_TK = 1024
WIN_KEYS = WINDOW + NSA_TQ


def _softmax_parts(s_t, bias_t):
    s_t = s_t + bias_t
    m = jnp.maximum(jnp.max(s_t, axis=0, keepdims=True), M_FLOOR)
    p = jnp.exp2(s_t - m)
    return p, jnp.sum(p, axis=0, keepdims=True)


def _safe_inv(l):
    return jnp.where(l > 0.0, 1.0 / l, 0.0)


def _top_n_membership(score_t):
    nblk = score_t.shape[0]
    j_col = lax.broadcasted_iota(jnp.int32, (nblk, 1), 0).astype(F32)
    sel = jnp.zeros(score_t.shape, F32)
    s = score_t
    for _ in range(N_SEL):
        top = jnp.max(s, axis=0, keepdims=True)
        first = jnp.min(jnp.where(s == top, j_col, float(nblk)), axis=0, keepdims=True)
        hit = j_col == first
        sel = jnp.where(hit, 1.0, sel)
        s = jnp.where(hit, -jnp.inf, s)
    return sel


def _nsa_kernel(q_ref, gate_ref, kc_ref, vct_ref, ks_ref, vst_ref, kw_ref, vwt_ref, qg_ref, o_ref, selt_ref):
    r_, tq, dh, tk = NSA_GROUP, NSA_TQ, NSA_HEAD_DIM, NSA_TK
    i = pl.program_id(1)
    start = i * tq
    ncmp = kc_ref.shape[0]
    nslc = selt_ref.shape[0]
    t_row = start + lax.broadcasted_iota(jnp.int32, (1, tq), 1)
    head = lambda x, r: x[:, r * tq:(r + 1) * tq]
    heads = lambda f: jnp.concatenate([f(r) for r in range(r_)], axis=1)

    q_t = heads(lambda r: _head_rmsnorm(q_ref[:, r * dh:(r + 1) * dh], qg_ref[...],
                                        dh ** -0.5 * LOG2E).T.astype(BF16))

    n_col = lax.broadcasted_iota(jnp.int32, (ncmp, 1), 0)
    bias_c = jnp.where(n_col * CMP_STRIDE + (CMP_LEN - 1) <= t_row, 0.0, NEG)
    s_t = jnp.dot(kc_ref[...], q_t, preferred_element_type=F32)
    p_c = []
    for r in range(r_):
        p, l = _softmax_parts(head(s_t, r), bias_c)
        p_c.append(p * _safe_inv(l))
    o_c = jnp.dot(vct_ref[...], heads(lambda r: p_c[r].astype(BF16)), preferred_element_type=F32)

    psum = functools.reduce(jnp.add, p_c)
    n_row = lax.broadcasted_iota(jnp.int32, (1, ncmp), 1) * CMP_STRIDE
    j_col = lax.broadcasted_iota(jnp.int32, (nslc, 1), 0)
    overlap = jnp.where((n_row < (j_col + 1) * SEL_BLOCK) & (n_row + CMP_LEN > j_col * SEL_BLOCK),
                        1.0, 0.0).astype(BF16)
    p_hi = psum.astype(BF16)
    p_lo = (psum - p_hi.astype(F32)).astype(BF16)
    imp = (jnp.dot(overlap, p_hi, preferred_element_type=F32)
           + jnp.dot(overlap, p_lo, preferred_element_type=F32))
    cur = t_row >> 6
    valid = j_col * SEL_BLOCK <= t_row
    forced = (j_col == 0) | (j_col == cur) | (j_col == cur - 1)
    score = jnp.where(valid, imp + jnp.where(forced, FORCE, 0.0), NEG)
    selt_ref[...] = _top_n_membership(score)

    lo = pl.multiple_of(jnp.maximum(start - WINDOW, 0), tq)
    kpos = lo + lax.broadcasted_iota(jnp.int32, (WIN_KEYS, 1), 0)
    bias_w = jnp.where((kpos <= t_row) & (kpos > t_row - WINDOW), 0.0, NEG)
    s_t = jnp.dot(kw_ref[pl.ds(lo, WIN_KEYS), :], q_t, preferred_element_type=F32)
    p_w, l_w = [], []
    for r in range(r_):
        p, l = _softmax_parts(head(s_t, r), bias_w)
        p_w.append(p.astype(BF16))
        l_w.append(l)
    p_w = jnp.concatenate(p_w, axis=1)
    blk0 = lo // tq
    o_w = functools.reduce(jnp.add, [
        jnp.dot(vwt_ref[blk0 + b], p_w[b * tq:(b + 1) * tq, :], preferred_element_type=F32)
        for b in range(WIN_KEYS // tq)]) * _safe_inv(jnp.concatenate(l_w, axis=1))

    k_col = lax.broadcasted_iota(jnp.int32, (SEL_BLOCK, 1), 0)
    blocks_per_tile = tk // SEL_BLOCK

    n_tiles = (start + tq + tk - 1) // tk
    cat = lambda xs: jnp.concatenate(xs, axis=1)
    pair_w = 2 * tq
    n_pair = r_ // 2

    def sel_body(c, carry):
        m, l, acc = carry
        k_tile = ks_ref[pl.ds(pl.multiple_of(c * tk, tk), tk), :]
        v_tile = vst_ref[c]
        bias = []
        for b in range(blocks_per_tile):
            member = selt_ref[pl.ds(c * blocks_per_tile + b, 1), :]
            causal = jnp.where(c * tk + b * SEL_BLOCK + k_col <= t_row, 0.0, NEG)
            bias.append(jnp.minimum(causal, jnp.where(member > 0.5, 0.0, NEG)))
        bias = jnp.concatenate(bias, axis=0)
        m_out, l_out, acc_out = [], [], []
        scores = [jnp.dot(k_tile, q_t[:, hp * pair_w:(hp + 1) * pair_w], preferred_element_type=F32)
                  for hp in range(n_pair)]
        for hp in range(n_pair):
            s2 = scores[hp]
            m_new, alpha, l_new, p_t = [], [], [], []
            for u in range(2):
                s = head(s2, u) + bias
                m_u = jnp.maximum(head(m[hp], u), jnp.max(s, axis=0, keepdims=True))
                a_u = jnp.exp2(head(m[hp], u) - m_u)
                p = jnp.exp2(s - m_u)
                m_new.append(m_u)
                alpha.append(a_u)
                l_new.append(a_u * head(l[hp], u) + jnp.sum(p, axis=0, keepdims=True))
                p_t.append(p.astype(BF16))
            m_out.append(cat(m_new))
            l_out.append(cat(l_new))
            acc_out.append(cat(alpha) * acc[hp] + jnp.dot(v_tile, cat(p_t), preferred_element_type=F32))
        return tuple(m_out), tuple(l_out), tuple(acc_out)

    init = (tuple(jnp.full((1, pair_w), M_FLOOR, F32) for _ in range(n_pair)),
            tuple(jnp.zeros((1, pair_w), F32) for _ in range(n_pair)),
            tuple(jnp.zeros((dh, pair_w), F32) for _ in range(n_pair)))
    _, l_s, acc_s = lax.fori_loop(0, n_tiles, sel_body, init)
    o_s = cat(acc_s) * _safe_inv(cat(l_s))

    gates = jax.nn.sigmoid(gate_ref[...]).T
    for r in range(r_):
        gate = lambda b: gates[b * GATE_STRIDE + r:b * GATE_STRIDE + r + 1, :]
        o = gate(0) * head(o_c, r) + gate(1) * head(o_s, r) + gate(2) * head(o_w, r)
        o_ref[:, r * dh:(r + 1) * dh] = o.T.astype(o_ref.dtype)


def _nsa_attention(h, h_tail, cmp_kv, kvn, q_gain):
    s = h.shape[0]
    g, dh = NSA_KV_HEADS, NSA_HEAD_DIM
    qw = NSA_GROUP * dh
    ncmp = cmp_kv.shape[2]
    gate_blk = X_W // LANES
    tk, tq = NSA_TK, NSA_TQ
    val_t = lambda part, blk: (kvn[:, part * NSA_KVW:(part + 1) * NSA_KVW]
                               .reshape(s // blk, blk, g, dh).transpose(2, 0, 3, 1))
    vc_t = jnp.swapaxes(cmp_kv[1], 1, 2)
    k_spec = lambda part: pl.BlockSpec((s, dh), lambda a, i: (0, part * g + a))
    v_spec = lambda blk: pl.BlockSpec((None, s // blk, dh, blk), lambda a, i: (a, 0, 0, 0))
    return pl.pallas_call(
        _nsa_kernel,
        grid=(g, s // tq),
        in_specs=[pl.BlockSpec((tq, qw), lambda a, i: (i, a)),
                  pl.BlockSpec((tq, LANES), lambda a, i: (i, gate_blk + a)),
                  pl.BlockSpec((None, None, ncmp, dh), lambda a, i: (0, a, 0, 0)),
                  pl.BlockSpec((None, dh, ncmp), lambda a, i: (a, 0, 0)),
                  k_spec(0), v_spec(tk), k_spec(2), v_spec(tq),
                  pl.BlockSpec((1, dh), lambda a, i: (0, 0))],
        out_specs=pl.BlockSpec((tq, qw), lambda a, i: (i, a)),
        out_shape=jax.ShapeDtypeStruct((s, MIX_W), BF16),
        scratch_shapes=[pltpu.VMEM((s // SEL_BLOCK, tq), F32)],
        compiler_params=_params(("parallel", "arbitrary")),
        name="nsa_attention",
    )(h, h_tail, cmp_kv, vc_t, kvn, val_t(1, tk), kvn, val_t(3, tq), q_gain.reshape(1, dh))


def _nsa_tail_weight(w_tail):
    k = w_tail.shape[0]
    gate_cols = jnp.zeros((k, NSA_KV_HEADS, LANES), w_tail.dtype)
    gsrc = w_tail[:, :NSA_GW].reshape(k, NSA_KV_HEADS, NSA_GROUP, 3)
    for b in range(3):
        gate_cols = gate_cols.at[:, :, b * GATE_STRIDE:b * GATE_STRIDE + NSA_GROUP].set(gsrc[..., b])
    return jnp.concatenate([w_tail[:, NSA_GW:], gate_cols.reshape(k, -1)], axis=1)


MM_TM = 1024
NORM_TM = 256


def kernel(x, mem, attn_norm, mem_norm, w_mem_kv, xq_gain, xk_gain, w_out, mlp_norm, w_up, w_down,
           cf_w_in, cf_conv_w, cf_conv_b, cf_ln_g, cf_ln_b,
           nsa_w_in, nsa_q_gain, nsa_k_gain, nsa_cmp_pos, nsa_cmp_w1, nsa_cmp_w2):
    b, s, d = x.shape
    assert b == 1 and d == D_MODEL and s % MM_TM == 0 and s // SEL_BLOCK == LANES
    x2 = x.reshape(s, d)
    mem2 = mem.reshape(mem.shape[1], d)
    n_mem = mem2.shape[0]
    mm = functools.partial(_matmul, tm=MM_TM, tk=d)
    down_w = w_down.astype(BF16)
    xg, ssq = _prenorm(x2, attn_norm[0], NORM_TM)
    for i in range(DEPTH):
        j = i // 2
        if i % 2 == 0:
            h = mm(xg, cf_w_in, layer=j, tn=512, out_dtype=F32, ssq=ssq)
            mix = _conformer_conv(h, cf_conv_w[j], cf_conv_b[j], cf_ln_g[j], cf_ln_b[j])
            h_x, qx_off = h, CF_QX_OFF
        else:
            h = mm(xg, nsa_w_in, layer=j, tn=512, n=NSA_MAIN_COLS, out_dtype=F32, ssq=ssq)
            h_x = mm(xg, _nsa_tail_weight(nsa_w_in[j, :, NSA_MAIN_COLS:]), tn=512, out_dtype=F32, ssq=ssq)
            cmp_kv = _nsa_compress(h, nsa_cmp_pos[j], nsa_cmp_w1[j], nsa_cmp_w2[j], nsa_k_gain[j])
            kvn = _nsa_kvprep(h, nsa_k_gain[j])
            mix = _nsa_attention(h, h_x, cmp_kv, kvn, nsa_q_gain[j])
            qx_off = 0
        memg, mem_ssq = _prenorm(mem2, mem_norm[i], n_mem)
        kv = _matmul(memg, w_mem_kv, layer=i, tm=n_mem, tn=1024, tk=d, out_dtype=F32, ssq=mem_ssq)
        xo = _memory_xattn(h_x, qx_off, kv, xq_gain[i], xk_gain[i])
        x2, xg, ssq = mm(mix, w_out, layer=i, a2=xo, tn=512, out_dtype=F32,
                         epilogue="residual", residual=x2, next_gain=mlp_norm[i])
        u = mm(xg, w_up, layer=i, tn=512, out_dtype=BF16, ssq=ssq, epilogue="relu2")
        down = functools.partial(_matmul, u, down_w, layer=i, tm=512, tn=256, tk=D_FF, out_dtype=F32,
                                 epilogue="residual", residual=x2)
        if i + 1 < DEPTH:
            x2, xg, ssq = down(next_gain=attn_norm[i + 1])
        else:
            x2 = down()
    return x2.reshape(b, s, d)
```

```python
import functools
import math

import jax
import jax.numpy as jnp
from jax import lax
from jax.experimental import pallas as pl
from jax.experimental.pallas import tpu as pltpu

D_MODEL = 4096
DEPTH = 4
MIX_W = 3 * D_MODEL // 4
X_W = D_MODEL // 4
X_HEADS = 4
X_HEAD_DIM = X_W // X_HEADS
CONV_CH = MIX_W
CONV_K = 31
NSA_HEAD_DIM = 128
NSA_HEADS = MIX_W // NSA_HEAD_DIM
NSA_KV_HEADS = 4
NSA_GROUP = NSA_HEADS // NSA_KV_HEADS
CMP_STRIDE = 16
CMP_LEN = 2 * CMP_STRIDE
CMP_HIDDEN = 256
SEL_BLOCK = 64
N_SEL = 16
WINDOW = 512
D_FF = 4 * D_MODEL
NSA_QW = NSA_HEADS * NSA_HEAD_DIM
NSA_KVW = NSA_KV_HEADS * NSA_HEAD_DIM
NSA_GW = NSA_HEADS * 3
NSA_COLS = NSA_QW + 6 * NSA_KVW + NSA_GW
EPS = 1e-6
NEG = -1e30
FORCE = 1e4

LANES = 128
SUBLANES = 8
BF16_SUBLANES = 16
VMEM_LIMIT = 56 * 1024 * 1024

M_FLOOR = -1e20
LOG2E = math.log2(math.e)

NSA_MAIN_COLS = NSA_QW + 6 * NSA_KVW
NSA_TAIL_COLS = X_W + NSA_KV_HEADS * LANES
GATE_STRIDE = 8
CF_QX_OFF = 2 * CONV_CH

F32 = jnp.float32
BF16 = jnp.bfloat16
_NT = (((1,), (1,)), ((), ()))


def _params(sem):
    return pltpu.CompilerParams(dimension_semantics=sem, vmem_limit_bytes=VMEM_LIMIT)


def _prenorm_kernel(x_ref, g_ref, xg_ref, ssq_ref):
    x = x_ref[...]
    xg_ref[...] = (x * g_ref[...]).astype(xg_ref.dtype)
    ssq_ref[...] = jnp.broadcast_to(jnp.sum(x * x, axis=-1, keepdims=True), ssq_ref.shape)


def _prenorm(x, g, tm):
    m, d = x.shape
    return pl.pallas_call(
        _prenorm_kernel,
        grid=(m // tm,),
        in_specs=[pl.BlockSpec((tm, d), lambda i: (i, 0)),
                  pl.BlockSpec((1, d), lambda i: (0, 0))],
        out_specs=[pl.BlockSpec((tm, d), lambda i: (i, 0)),
                   pl.BlockSpec((tm, LANES), lambda i: (i, 0))],
        out_shape=[jax.ShapeDtypeStruct((m, d), BF16), jax.ShapeDtypeStruct((m, LANES), F32)],
        compiler_params=_params(("parallel",)),
        name="prenorm",
    )(x, g.reshape(1, d))


def _mm_kernel(*refs, nk, kdim, two_a, rowscale, epilogue, emit_norm):
    refs = list(refs)
    a_ref, w_ref = refs[:2]
    pos = 2
    a2_ref = ssq_in = r_ref = g_ref = xg_ref = ssq_out = None
    if two_a:
        a2_ref = refs[pos]
        pos += 1
    if rowscale:
        ssq_in = refs[pos]
        pos += 1
    if epilogue == "residual":
        r_ref = refs[pos]
        pos += 1
        if emit_norm:
            g_ref = refs[pos]
            pos += 1
    o_ref = refs[pos]
    pos += 1
    if emit_norm:
        xg_ref, ssq_out = refs[pos:pos + 2]
        pos += 2
    scratch = refs[pos:]

    def finish(acc):
        if rowscale:
            acc = acc * lax.rsqrt(ssq_in[:, 0:1] * (1.0 / kdim) + EPS)
        if epilogue == "residual":
            acc = acc + r_ref[...]
        elif epilogue == "relu2":
            acc = jnp.square(jnp.maximum(acc, 0.0))
        o_ref[...] = acc.astype(o_ref.dtype)
        if emit_norm:
            xg_ref[...] = (acc * g_ref[...]).astype(xg_ref.dtype)
            part = jnp.broadcast_to(jnp.sum(acc * acc, axis=-1, keepdims=True), ssq_out.shape)
            j = pl.program_id(1)

            @pl.when(j == 0)
            def _():
                ssq_out[...] = part

            @pl.when(j > 0)
            def _():
                ssq_out[...] += part

    if two_a:
        k1 = a_ref.shape[1]
        part = (jnp.dot(a_ref[...], w_ref[0:k1, :].astype(BF16), preferred_element_type=F32)
                + jnp.dot(a2_ref[...], w_ref[k1:, :].astype(BF16), preferred_element_type=F32))
    else:
        part = jnp.dot(a_ref[...], w_ref[...].astype(BF16), preferred_element_type=F32)
    if nk == 1:
        finish(part)
        return
    acc_ref, = scratch
    k = pl.program_id(2)

    @pl.when(k == 0)
    def _():
        acc_ref[...] = part

    @pl.when(jnp.logical_and(k > 0, k < nk - 1))
    def _():
        acc_ref[...] += part

    @pl.when(k == nk - 1)
    def _():
        finish(acc_ref[...] + part)


def _matmul(a, w, *, tm, tn, tk, out_dtype, layer=None, n=None, a2=None, ssq=None, epilogue=None,
            residual=None, next_gain=None):
    m, kdim = a.shape
    n = w.shape[-1] if n is None else n
    if layer is None:
        w_spec = pl.BlockSpec((tk, tn), lambda i, j, k: (k, j))
    else:
        w_spec = pl.BlockSpec((None, tk, tn), lambda i, j, k: (layer, k, j))
    args = [a, w]
    if a2 is None:
        in_specs = [pl.BlockSpec((tm, tk), lambda i, j, k: (i, k)), w_spec]
    else:
        assert tk == kdim + a2.shape[1]
        in_specs = [pl.BlockSpec((tm, kdim), lambda i, j, k: (i, 0)), w_spec,
                    pl.BlockSpec((tm, a2.shape[1]), lambda i, j, k: (i, 0))]
        args.append(a2)
        kdim = tk
    nk = kdim // tk
    emit_norm = next_gain is not None
    if ssq is not None:
        in_specs.append(pl.BlockSpec((tm, LANES), lambda i, j, k: (i, 0)))
        args.append(ssq)
    if epilogue == "residual":
        in_specs.append(pl.BlockSpec((tm, tn), lambda i, j, k: (i, j)))
        args.append(residual)
        if emit_norm:
            in_specs.append(pl.BlockSpec((1, tn), lambda i, j, k: (0, j)))
            args.append(next_gain.reshape(1, n))
    out_specs = [pl.BlockSpec((tm, tn), lambda i, j, k: (i, j))]
    out_shape = [jax.ShapeDtypeStruct((m, n), out_dtype)]
    if emit_norm:
        out_specs += [pl.BlockSpec((tm, tn), lambda i, j, k: (i, j)),
                      pl.BlockSpec((tm, LANES), lambda i, j, k: (i, 0))]
        out_shape += [jax.ShapeDtypeStruct((m, n), BF16), jax.ShapeDtypeStruct((m, LANES), F32)]
    scratch = [pltpu.VMEM((tm, tn), F32)] if nk > 1 else []
    out = pl.pallas_call(
        functools.partial(_mm_kernel, nk=nk, kdim=kdim, two_a=a2 is not None, rowscale=ssq is not None,
                          epilogue=epilogue, emit_norm=emit_norm),
        grid=(m // tm, n // tn, nk),
        in_specs=in_specs,
        out_specs=out_specs,
        out_shape=out_shape,
        scratch_shapes=scratch,
        compiler_params=_params(("parallel", "arbitrary", "arbitrary")),
        name="matmul_" + (epilogue or "plain"),
    )(*args)
    return out if emit_norm else out[0]


CONV_TS = 256
CONV_HALO = 32
CONV_CW = 128


def _conv_kernel(a_ref, b_ref, w_ref, cb_ref, g_ref, bb_ref, o_ref, zbuf, ybuf):
    i = pl.program_id(0)
    ts = CONV_TS

    @pl.when(i == 0)
    def _():
        zbuf[0:CONV_HALO, :] = jnp.zeros((CONV_HALO, CONV_CH), F32)
        zbuf[CONV_HALO + ts:, :] = jnp.zeros((SUBLANES, CONV_CH), F32)

    @pl.when(i > 0)
    def _():
        zbuf[0:CONV_HALO, :] = zbuf[ts:ts + CONV_HALO, :]

    zbuf[CONV_HALO:CONV_HALO + ts, :] = a_ref[...] * jax.nn.sigmoid(b_ref[...])

    base = CONV_HALO - (CONV_K - 1)

    def chunk(c, carry):
        cols = pl.ds(pl.multiple_of(c * CONV_CW, CONV_CW), CONV_CW)
        acc = jnp.zeros((ts, CONV_CW), F32)
        for phase in range(SUBLANES):
            part = jnp.zeros((ts + SUBLANES, CONV_CW), F32)
            for k in range(CONV_K):
                if (base + k) % SUBLANES == phase:
                    off = base + k - phase
                    part = part + w_ref[k:k + 1, cols] * zbuf[off:off + ts + SUBLANES, cols]
            acc = acc + part[phase:phase + ts, :]
        ybuf[:, cols] = acc + cb_ref[:, cols]
        return carry

    lax.fori_loop(0, CONV_CH // CONV_CW, chunk, 0)

    y = ybuf[...]
    mu = jnp.mean(y, axis=-1, keepdims=True)
    yc = y - mu
    var = jnp.mean(yc * yc, axis=-1, keepdims=True)
    yn = yc * lax.rsqrt(var + EPS) * g_ref[...] + bb_ref[...]
    o_ref[...] = (yn * jax.nn.sigmoid(yn)).astype(o_ref.dtype)


def _conformer_conv(h, conv_w, conv_b, ln_g, ln_b):
    s = h.shape[0]
    c = CONV_CH
    wpad = jnp.zeros((CONV_HALO, c), F32).at[:CONV_K].set(conv_w)
    row = lambda v: v.reshape(1, c)
    full = lambda r: pl.BlockSpec((r, c), lambda i: (0, 0))
    return pl.pallas_call(
        _conv_kernel,
        grid=(s // CONV_TS,),
        in_specs=[pl.BlockSpec((CONV_TS, c), lambda i: (i, 0)),
                  pl.BlockSpec((CONV_TS, c), lambda i: (i, 1)),
                  full(CONV_HALO), full(1), full(1), full(1)],
        out_specs=pl.BlockSpec((CONV_TS, c), lambda i: (i, 0)),
        out_shape=jax.ShapeDtypeStruct((s, MIX_W), BF16),
        scratch_shapes=[pltpu.VMEM((CONV_TS + CONV_HALO + SUBLANES, c), F32),
                        pltpu.VMEM((CONV_TS, c), F32)],
        compiler_params=_params(("arbitrary",)),
        name="conformer_conv",
    )(h, h, wpad, row(conv_b), row(ln_g), row(ln_b))


XATTN_TS = 512


def _head_rmsnorm(x, gain, scale=1.0):
    ms = jnp.mean(x * x, axis=-1, keepdims=True)
    return x * (lax.rsqrt(ms + EPS) * scale) * gain


def _xattn_kernel(q_ref, kv_ref, qg_ref, kg_ref, o_ref):
    dh = X_HEAD_DIM
    for h in range(X_HEADS):
        q = _head_rmsnorm(q_ref[:, h * dh:(h + 1) * dh], qg_ref[...], dh ** -0.5 * LOG2E).astype(BF16)
        k = _head_rmsnorm(kv_ref[:, h * dh:(h + 1) * dh], kg_ref[...]).astype(BF16)
        v = kv_ref[:, X_W + h * dh:X_W + (h + 1) * dh].astype(BF16)
        s = lax.dot_general(q, k, _NT, preferred_element_type=F32)
        p = jnp.exp2(s - jnp.max(s, axis=-1, keepdims=True))
        l = jnp.sum(p, axis=-1, keepdims=True)
        o = jnp.dot(p.astype(BF16), v, preferred_element_type=F32) / l
        o_ref[:, h * dh:(h + 1) * dh] = o.astype(o_ref.dtype)


def _memory_xattn(h, qx_off, kv, q_gain, k_gain):
    s = h.shape[0]
    n_mem = kv.shape[0]
    qblk = qx_off // X_W
    return pl.pallas_call(
        _xattn_kernel,
        grid=(s // XATTN_TS,),
        in_specs=[pl.BlockSpec((XATTN_TS, X_W), lambda i: (i, qblk)),
                  pl.BlockSpec((n_mem, 2 * X_W), lambda i: (0, 0)),
                  pl.BlockSpec((1, X_HEAD_DIM), lambda i: (0, 0)),
                  pl.BlockSpec((1, X_HEAD_DIM), lambda i: (0, 0))],
        out_specs=pl.BlockSpec((XATTN_TS, X_W), lambda i: (i, 0)),
        out_shape=jax.ShapeDtypeStruct((s, X_W), BF16),
        compiler_params=_params(("parallel",)),
        name="memory_xattn",
    )(h, kv, q_gain.reshape(1, -1), k_gain.reshape(1, -1))


KVPREP_TS = 512


def _kvprep_kernel(x_ref, g_ref, o_ref):
    dh = NSA_HEAD_DIM
    for part in range(4):
        for g in range(NSA_KV_HEADS):
            lo = part * NSA_KVW + g * dh
            x = x_ref[:, lo:lo + dh]
            if part % 2 == 0:
                x = _head_rmsnorm(x, g_ref[part // 2:part // 2 + 1, :])
            o_ref[:, lo:lo + dh] = x.astype(o_ref.dtype)


def _nsa_kvprep(h, k_gain):
    s = h.shape[0]
    w = 4 * NSA_KVW
    blk = (NSA_QW + 2 * NSA_KVW) // w
    return pl.pallas_call(
        _kvprep_kernel,
        grid=(s // KVPREP_TS,),
        in_specs=[pl.BlockSpec((KVPREP_TS, w), lambda i: (i, blk)),
                  pl.BlockSpec((2, NSA_HEAD_DIM), lambda i: (0, 0))],
        out_specs=pl.BlockSpec((KVPREP_TS, w), lambda i: (i, 0)),
        out_shape=jax.ShapeDtypeStruct((s, w), BF16),
        compiler_params=_params(("parallel",)),
        name="nsa_kvprep",
    )(h, k_gain[1:3])


def _compress_kernel(c_ref, pa_ref, pb_ref, w1_ref, w2_ref, kg_ref, o_ref):
    half = CMP_STRIDE * NSA_HEAD_DIM
    c = c_ref[...]
    first = jnp.dot((c + pa_ref[...]).astype(BF16), w1_ref[0:half, :], preferred_element_type=F32)
    second = jnp.dot((c + pb_ref[...]).astype(BF16), w1_ref[half:2 * half, :], preferred_element_type=F32)
    nchunk = c.shape[0]
    hid = first + pltpu.roll(second, nchunk - 1, 0)
    hid = hid * jax.nn.sigmoid(hid)
    out = jnp.dot(hid.astype(BF16), w2_ref[...], preferred_element_type=F32)
    normed = _head_rmsnorm(out, kg_ref[...])
    is_key = pl.program_id(0) == 0
    o_ref[...] = jnp.where(is_key, normed, out).astype(o_ref.dtype)


def _nsa_compress(h, cmp_pos, cmp_w1, cmp_w2, k_gain):
    s = h.shape[0]
    g, dh = NSA_KV_HEADS, NSA_HEAD_DIM
    nchunk = s // CMP_STRIDE
    half = CMP_STRIDE * dh
    c = h[:, NSA_QW:NSA_QW + 2 * NSA_KVW].reshape(s, 2, g, dh).transpose(1, 2, 0, 3).reshape(2, g, nchunk, half)
    pos = cmp_pos.reshape(2, 2, 1, half)
    return pl.pallas_call(
        _compress_kernel,
        grid=(2, g),
        in_specs=[pl.BlockSpec((None, None, nchunk, half), lambda a, b: (a, b, 0, 0)),
                  pl.BlockSpec((None, None, 1, half), lambda a, b: (a, 0, 0, 0)),
                  pl.BlockSpec((None, None, 1, half), lambda a, b: (a, 1, 0, 0)),
                  pl.BlockSpec((None, 2 * half, CMP_HIDDEN), lambda a, b: (a, 0, 0)),
                  pl.BlockSpec((None, CMP_HIDDEN, dh), lambda a, b: (a, 0, 0)),
                  pl.BlockSpec((1, dh), lambda a, b: (0, 0))],
        out_specs=pl.BlockSpec((None, None, nchunk, dh), lambda a, b: (a, b, 0, 0)),
        out_shape=jax.ShapeDtypeStruct((2, g, nchunk, dh), BF16),
        compiler_params=_params(("parallel", "parallel")),
        name="nsa_compress",
    )(c, pos, pos, cmp_w1.astype(BF16), cmp_w2.astype(BF16), k_gain[0:1])


NSA_TQ = 128
NSA_TK = 1024
WIN_KEYS = WINDOW + NSA_TQ


def _softmax_parts(s_t, bias_t):
    s_t = s_t + bias_t
    m = jnp.maximum(jnp.max(s_t, axis=0, keepdims=True), M_FLOOR)
    p = jnp.exp2(s_t - m)
    return p, jnp.sum(p, axis=0, keepdims=True)


def _safe_inv(l):
    return jnp.where(l > 0.0, 1.0 / l, 0.0)


def _top_n_membership(score_t):
    nblk = score_t.shape[0]
    j_col = lax.broadcasted_iota(jnp.int32, (nblk, 1), 0).astype(F32)
    sel = jnp.zeros(score_t.shape, F32)
    s = score_t
    for _ in range(N_SEL):
        top = jnp.max(s, axis=0, keepdims=True)
        first = jnp.min(jnp.where(s == top, j_col, float(nblk)), axis=0, keepdims=True)
        hit = j_col == first
        sel = jnp.where(hit, 1.0, sel)
        s = jnp.where(hit, -jnp.inf, s)
    return sel


def _nsa_kernel(q_ref, gate_ref, kc_ref, vct_ref, ks_ref, vst_ref, kw_ref, vwt_ref, qg_ref, oh_ref, o_ref,
                selt_ref, p_ref, qaug_ref):
    r_, tq, dh, tk = NSA_GROUP, NSA_TQ, NSA_HEAD_DIM, NSA_TK
    i = pl.program_id(1)
    start = i * tq
    ncmp = kc_ref.shape[0]
    nslc = selt_ref.shape[0]
    t_row = start + lax.broadcasted_iota(jnp.int32, (1, tq), 1)
    head = lambda x, r: x[:, r * tq:(r + 1) * tq]
    heads = lambda f: jnp.concatenate([f(r) for r in range(r_)], axis=1)

    q_t = heads(lambda r: _head_rmsnorm(q_ref[:, r * dh:(r + 1) * dh], qg_ref[...],
                                        dh ** -0.5 * LOG2E).T.astype(BF16))

    n_col = lax.broadcasted_iota(jnp.int32, (ncmp, 1), 0)
    bias_c = jnp.where(n_col * CMP_STRIDE + (CMP_LEN - 1) <= t_row, 0.0, NEG)
    s_t = jnp.dot(kc_ref[...], q_t, preferred_element_type=F32)
    p_c = []
    for r in range(r_):
        p, l = _softmax_parts(head(s_t, r), bias_c)
        p_c.append(p * _safe_inv(l))
    o_c = jnp.dot(vct_ref[...], heads(lambda r: p_c[r].astype(BF16)), preferred_element_type=F32)

    psum = functools.reduce(jnp.add, p_c)
    n_row = lax.broadcasted_iota(jnp.int32, (1, ncmp), 1) * CMP_STRIDE
    j_col = lax.broadcasted_iota(jnp.int32, (nslc, 1), 0)
    overlap = jnp.where((n_row < (j_col + 1) * SEL_BLOCK) & (n_row + CMP_LEN > j_col * SEL_BLOCK),
                        1.0, 0.0).astype(BF16)
    p_hi = psum.astype(BF16)
    p_lo = (psum - p_hi.astype(F32)).astype(BF16)
    imp = (jnp.dot(overlap, p_hi, preferred_element_type=F32)
           + jnp.dot(overlap, p_lo, preferred_element_type=F32))
    cur = t_row >> 6
    valid = j_col * SEL_BLOCK <= t_row
    forced = (j_col == 0) | (j_col == cur) | (j_col == cur - 1)
    score = jnp.where(valid, imp + jnp.where(forced, FORCE, 0.0), NEG)
    selt_ref[...] = _top_n_membership(score)

    lo = pl.multiple_of(jnp.maximum(start - WINDOW, 0), tq)
    kpos = lo + lax.broadcasted_iota(jnp.int32, (WIN_KEYS, 1), 0)
    bias_w = jnp.where((kpos <= t_row) & (kpos > t_row - WINDOW), 0.0, NEG)
    s_t = jnp.dot(kw_ref[pl.ds(lo, WIN_KEYS), :], q_t, preferred_element_type=F32)
    p_w, l_w = [], []
    for r in range(r_):
        p, l = _softmax_parts(head(s_t, r), bias_w)
        p_w.append(p.astype(BF16))
        l_w.append(l)
    p_w = jnp.concatenate(p_w, axis=1)
    blk0 = lo // tq
    o_w = functools.reduce(jnp.add, [
        jnp.dot(vwt_ref[blk0 + b], p_w[b * tq:(b + 1) * tq, :], preferred_element_type=F32)
        for b in range(WIN_KEYS // tq)]) * _safe_inv(jnp.concatenate(l_w, axis=1))

    blocks_per_tile = tk // SEL_BLOCK
    n_tiles = (start + tq + tk - 1) // tk
    cat = lambda xs: jnp.concatenate(xs, axis=1)
    pair_w = 2 * tq
    n_pair = r_ // 2
    pair = lambda hp: slice(hp * pair_w, (hp + 1) * pair_w)
    vrows = vst_ref.shape[1]

    qaug_ref[0:dh, :] = q_t
    qaug_ref[dh + blocks_per_tile:, :] = jnp.zeros((qaug_ref.shape[0] - dh - blocks_per_tile, r_ * tq), BF16)
    p_ref[...] = jnp.zeros(p_ref.shape, p_ref.dtype)

    def values(c, alpha, acc):
        v_tile = vst_ref[c]
        return tuple(alpha[hp] * acc[hp] + jnp.dot(v_tile, p_ref[:, pair(hp)], preferred_element_type=F32)
                     for hp in range(n_pair))

    def tile_step(c, carry, diagonal):
        m, acc, a_prev = carry
        member = selt_ref[pl.ds(pl.multiple_of(c * blocks_per_tile, blocks_per_tile), blocks_per_tile), :]
        qaug_ref[dh:dh + blocks_per_tile, :] = cat([jnp.where(member > 0.5, 0.0, NEG).astype(BF16)] * r_)
        k_aug = jnp.concatenate([ks_ref[pl.ds(pl.multiple_of(c * tk, tk), tk), :], oh_ref[...]], axis=1)
        scores = [jnp.dot(k_aug, qaug_ref[:, pair(hp)], preferred_element_type=F32) for hp in range(n_pair)]
        acc = values(jnp.maximum(c - 1, 0), a_prev, acc)
        if diagonal:
            k_col = c * tk + lax.broadcasted_iota(jnp.int32, (tk, 1), 0)
            causal = jnp.where(k_col <= t_row, 0.0, NEG)
        m_out, a_out = [], []
        for hp in range(n_pair):
            m_new, alpha = [], []
            for u in range(2):
                s = head(scores[hp], u)
                if diagonal:
                    s = s + causal
                m_u = jnp.maximum(head(m[hp], u), jnp.max(s, axis=0, keepdims=True))
                m_new.append(m_u)
                alpha.append(jnp.exp2(head(m[hp], u) - m_u))
                p_ref[:, hp * pair_w + u * tq:hp * pair_w + (u + 1) * tq] = jnp.exp2(s - m_u).astype(BF16)
            m_out.append(cat(m_new))
            a_out.append(cat(alpha))
        return tuple(m_out), acc, tuple(a_out)

    per_pair = lambda shape, v: tuple(jnp.full(shape, v, F32) for _ in range(n_pair))
    init = (per_pair((1, pair_w), M_FLOOR), per_pair((vrows, pair_w), 0.0), per_pair((1, pair_w), 1.0))
    carry = lax.fori_loop(0, n_tiles - 1, functools.partial(tile_step, diagonal=False), init)
    _, acc_s, a_last = tile_step(n_tiles - 1, carry, diagonal=True)
    acc_s = cat(values(n_tiles - 1, a_last, acc_s))
    o_s = acc_s[0:dh, :] * _safe_inv(acc_s[dh:dh + 1, :])

    gates = jax.nn.sigmoid(gate_ref[...]).T
    for r in range(r_):
        gate = lambda b: gates[b * GATE_STRIDE + r:b * GATE_STRIDE + r + 1, :]
        o = gate(0) * head(o_c, r) + gate(1) * head(o_s, r) + gate(2) * head(o_w, r)
        o_ref[:, r * dh:(r + 1) * dh] = o.T.astype(o_ref.dtype)


def _nsa_attention(h, h_tail, cmp_kv, kvn, q_gain):
    s = h.shape[0]
    g, dh = NSA_KV_HEADS, NSA_HEAD_DIM
    qw = NSA_GROUP * dh
    ncmp = cmp_kv.shape[2]
    gate_blk = X_W // LANES
    tk, tq = NSA_TK, NSA_TQ
    val_t = lambda part, blk: (kvn[:, part * NSA_KVW:(part + 1) * NSA_KVW]
                               .reshape(s // blk, blk, g, dh).transpose(2, 0, 3, 1))
    vc_t = jnp.swapaxes(cmp_kv[1], 1, 2)
    vs_t = val_t(1, tk)
    ones_rows = jnp.zeros((g, s // tk, BF16_SUBLANES, tk), BF16).at[:, :, 0, :].set(1.0)
    vs_t = jnp.concatenate([vs_t, ones_rows], axis=2)
    vrows = dh + BF16_SUBLANES
    block_onehot = (lax.broadcasted_iota(jnp.int32, (tk, LANES), 0) // SEL_BLOCK
                    == lax.broadcasted_iota(jnp.int32, (tk, LANES), 1)).astype(BF16)
    k_spec = lambda part: pl.BlockSpec((s, dh), lambda a, i: (0, part * g + a))
    v_spec = lambda rows, blk: pl.BlockSpec((None, s // blk, rows, blk), lambda a, i: (a, 0, 0, 0))
    return pl.pallas_call(
        _nsa_kernel,
        grid=(g, s // tq),
        in_specs=[pl.BlockSpec((tq, qw), lambda a, i: (i, a)),
                  pl.BlockSpec((tq, LANES), lambda a, i: (i, gate_blk + a)),
                  pl.BlockSpec((None, None, ncmp, dh), lambda a, i: (0, a, 0, 0)),
                  pl.BlockSpec((None, dh, ncmp), lambda a, i: (a, 0, 0)),
                  k_spec(0), v_spec(vrows, tk), k_spec(2), v_spec(dh, tq),
                  pl.BlockSpec((1, dh), lambda a, i: (0, 0)),
                  pl.BlockSpec((tk, LANES), lambda a, i: (0, 0))],
        out_specs=pl.BlockSpec((tq, qw), lambda a, i: (i, a)),
        out_shape=jax.ShapeDtypeStruct((s, MIX_W), BF16),
        scratch_shapes=[pltpu.VMEM((s // SEL_BLOCK, tq), F32),
                        pltpu.VMEM((tk, NSA_GROUP * tq), BF16),
                        pltpu.VMEM((dh + LANES, NSA_GROUP * tq), BF16)],
        compiler_params=_params(("parallel", "arbitrary")),
        name="nsa_attention",
    )(h, h_tail, cmp_kv, vc_t, kvn, vs_t, kvn, val_t(3, tq), q_gain.reshape(1, dh), block_onehot)


def _nsa_tail_weight(w_tail):
    k = w_tail.shape[0]
    gate_cols = jnp.zeros((k, NSA_KV_HEADS, LANES), w_tail.dtype)
    gsrc = w_tail[:, :NSA_GW].reshape(k, NSA_KV_HEADS, NSA_GROUP, 3)
    for b in range(3):
        gate_cols = gate_cols.at[:, :, b * GATE_STRIDE:b * GATE_STRIDE + NSA_GROUP].set(gsrc[..., b])
    return jnp.concatenate([w_tail[:, NSA_GW:], gate_cols.reshape(k, -1)], axis=1)


MM_TM = 1024
NORM_TM = 256


def kernel(x, mem, attn_norm, mem_norm, w_mem_kv, xq_gain, xk_gain, w_out, mlp_norm, w_up, w_down,
           cf_w_in, cf_conv_w, cf_conv_b, cf_ln_g, cf_ln_b,
           nsa_w_in, nsa_q_gain, nsa_k_gain, nsa_cmp_pos, nsa_cmp_w1, nsa_cmp_w2):
    b, s, d = x.shape
    assert b == 1 and d == D_MODEL and s % MM_TM == 0 and s // SEL_BLOCK == LANES
    x2 = x.reshape(s, d)
    mem2 = mem.reshape(mem.shape[1], d)
    n_mem = mem2.shape[0]
    mm = functools.partial(_matmul, tm=MM_TM, tk=d)
    xg, ssq = _prenorm(x2, attn_norm[0], NORM_TM)
    for i in range(DEPTH):
        j = i // 2
        if i % 2 == 0:
            h = mm(xg, cf_w_in, layer=j, tn=512, out_dtype=F32, ssq=ssq)
            mix = _conformer_conv(h, cf_conv_w[j], cf_conv_b[j], cf_ln_g[j], cf_ln_b[j])
            h_x, qx_off = h, CF_QX_OFF
        else:
            h = mm(xg, nsa_w_in, layer=j, tn=512, n=NSA_MAIN_COLS, out_dtype=F32, ssq=ssq)
            h_x = mm(xg, _nsa_tail_weight(nsa_w_in[j, :, NSA_MAIN_COLS:]), tn=512, out_dtype=F32, ssq=ssq)
            cmp_kv = _nsa_compress(h, nsa_cmp_pos[j], nsa_cmp_w1[j], nsa_cmp_w2[j], nsa_k_gain[j])
            kvn = _nsa_kvprep(h, nsa_k_gain[j])
            mix = _nsa_attention(h, h_x, cmp_kv, kvn, nsa_q_gain[j])
            qx_off = 0
        memg, mem_ssq = _prenorm(mem2, mem_norm[i], n_mem)
        kv = _matmul(memg, w_mem_kv, layer=i, tm=n_mem, tn=1024, tk=d, out_dtype=F32, ssq=mem_ssq)
        xo = _memory_xattn(h_x, qx_off, kv, xq_gain[i], xk_gain[i])
        x2, xg, ssq = mm(mix, w_out, layer=i, a2=xo, tn=512, out_dtype=F32,
                         epilogue="residual", residual=x2, next_gain=mlp_norm[i])
        u = mm(xg, w_up, layer=i, tn=512, out_dtype=BF16, ssq=ssq, epilogue="relu2")
        down = functools.partial(_matmul, u, w_down, layer=i, tm=MM_TM, tn=512, tk=d, out_dtype=F32,
                                 epilogue="residual", residual=x2)
        if i + 1 < DEPTH:
            x2, xg, ssq = down(next_gain=attn_norm[i + 1])
        else:
            x2 = down()
    return x2.reshape(b, s, d)
```

```python
import functools
import math

import jax
import jax.numpy as jnp
from jax import lax
from jax.experimental import pallas as pl
from jax.experimental.pallas import tpu as pltpu

D_MODEL = 4096
DEPTH = 4
MIX_W = 3 * D_MODEL // 4
X_W = D_MODEL // 4
X_HEADS = 4
X_HEAD_DIM = X_W // X_HEADS
CONV_CH = MIX_W
CONV_K = 31
NSA_HEAD_DIM = 128
NSA_HEADS = MIX_W // NSA_HEAD_DIM
NSA_KV_HEADS = 4
NSA_GROUP = NSA_HEADS // NSA_KV_HEADS
CMP_STRIDE = 16
CMP_LEN = 2 * CMP_STRIDE
CMP_HIDDEN = 256
SEL_BLOCK = 64
N_SEL = 16
WINDOW = 512
D_FF = 4 * D_MODEL
NSA_QW = NSA_HEADS * NSA_HEAD_DIM
NSA_KVW = NSA_KV_HEADS * NSA_HEAD_DIM
NSA_GW = NSA_HEADS * 3
NSA_COLS = NSA_QW + 6 * NSA_KVW + NSA_GW
EPS = 1e-6
NEG = -1e30
FORCE = 1e4

LANES = 128
SUBLANES = 8
BF16_SUBLANES = 16
VMEM_LIMIT = 56 * 1024 * 1024

M_FLOOR = -1e20
LOG2E = math.log2(math.e)

NSA_MAIN_COLS = NSA_QW + 6 * NSA_KVW
NSA_TAIL_COLS = X_W + NSA_KV_HEADS * LANES
GATE_STRIDE = 8
CF_QX_OFF = 2 * CONV_CH

F32 = jnp.float32
BF16 = jnp.bfloat16
_NT = (((1,), (1,)), ((), ()))


def _params(sem):
    return pltpu.CompilerParams(dimension_semantics=sem, vmem_limit_bytes=VMEM_LIMIT)


def _prenorm_kernel(x_ref, g_ref, xg_ref, ssq_ref):
    x = x_ref[...]
    xg_ref[...] = (x * g_ref[...]).astype(xg_ref.dtype)
    ssq_ref[...] = jnp.broadcast_to(jnp.sum(x * x, axis=-1, keepdims=True), ssq_ref.shape)


def _prenorm(x, g, tm):
    m, d = x.shape
    return pl.pallas_call(
        _prenorm_kernel,
        grid=(m // tm,),
        in_specs=[pl.BlockSpec((tm, d), lambda i: (i, 0)),
                  pl.BlockSpec((1, d), lambda i: (0, 0))],
        out_specs=[pl.BlockSpec((tm, d), lambda i: (i, 0)),
                   pl.BlockSpec((tm, LANES), lambda i: (i, 0))],
        out_shape=[jax.ShapeDtypeStruct((m, d), BF16), jax.ShapeDtypeStruct((m, LANES), F32)],
        compiler_params=_params(("parallel",)),
        name="prenorm",
    )(x, g.reshape(1, d))


def _mm_kernel(*refs, nk, kdim, two_a, rowscale, epilogue, emit_norm):
    refs = list(refs)
    a_ref, w_ref = refs[:2]
    pos = 2
    a2_ref = ssq_in = r_ref = g_ref = xg_ref = ssq_out = None
    if two_a:
        a2_ref = refs[pos]
        pos += 1
    if rowscale:
        ssq_in = refs[pos]
        pos += 1
    if epilogue == "residual":
        r_ref = refs[pos]
        pos += 1
        if emit_norm:
            g_ref = refs[pos]
            pos += 1
    o_ref = refs[pos]
    pos += 1
    if emit_norm:
        xg_ref, ssq_out = refs[pos:pos + 2]
        pos += 2
    scratch = refs[pos:]

    def finish(acc):
        if rowscale:
            acc = acc * lax.rsqrt(ssq_in[:, 0:1] * (1.0 / kdim) + EPS)
        if epilogue == "residual":
            acc = acc + r_ref[...]
        elif epilogue == "relu2":
            acc = jnp.square(jnp.maximum(acc, 0.0))
        o_ref[...] = acc.astype(o_ref.dtype)
        if emit_norm:
            xg_ref[...] = (acc * g_ref[...]).astype(xg_ref.dtype)
            part = jnp.broadcast_to(jnp.sum(acc * acc, axis=-1, keepdims=True), ssq_out.shape)
            j = pl.program_id(1)

            @pl.when(j == 0)
            def _():
                ssq_out[...] = part

            @pl.when(j > 0)
            def _():
                ssq_out[...] += part

    if two_a:
        k1 = a_ref.shape[1]
        part = (jnp.dot(a_ref[...], w_ref[0:k1, :].astype(BF16), preferred_element_type=F32)
                + jnp.dot(a2_ref[...], w_ref[k1:, :].astype(BF16), preferred_element_type=F32))
    else:
        part = jnp.dot(a_ref[...], w_ref[...].astype(BF16), preferred_element_type=F32)
    if nk == 1:
        finish(part)
        return
    acc_ref, = scratch
    k = pl.program_id(2)

    @pl.when(k == 0)
    def _():
        acc_ref[...] = part

    @pl.when(jnp.logical_and(k > 0, k < nk - 1))
    def _():
        acc_ref[...] += part

    @pl.when(k == nk - 1)
    def _():
        finish(acc_ref[...] + part)


def _matmul(a, w, *, tm, tn, tk, out_dtype, layer=None, n=None, a2=None, ssq=None, epilogue=None,
            residual=None, next_gain=None):
    m, kdim = a.shape
    n = w.shape[-1] if n is None else n
    if layer is None:
        w_spec = pl.BlockSpec((tk, tn), lambda i, j, k: (k, j))
    else:
        w_spec = pl.BlockSpec((None, tk, tn), lambda i, j, k: (layer, k, j))
    args = [a, w]
    if a2 is None:
        in_specs = [pl.BlockSpec((tm, tk), lambda i, j, k: (i, k)), w_spec]
    else:
        assert tk == kdim + a2.shape[1]
        in_specs = [pl.BlockSpec((tm, kdim), lambda i, j, k: (i, 0)), w_spec,
                    pl.BlockSpec((tm, a2.shape[1]), lambda i, j, k: (i, 0))]
        args.append(a2)
        kdim = tk
    nk = kdim // tk
    emit_norm = next_gain is not None
    if ssq is not None:
        in_specs.append(pl.BlockSpec((tm, LANES), lambda i, j, k: (i, 0)))
        args.append(ssq)
    if epilogue == "residual":
        in_specs.append(pl.BlockSpec((tm, tn), lambda i, j, k: (i, j)))
        args.append(residual)
        if emit_norm:
            in_specs.append(pl.BlockSpec((1, tn), lambda i, j, k: (0, j)))
            args.append(next_gain.reshape(1, n))
    out_specs = [pl.BlockSpec((tm, tn), lambda i, j, k: (i, j))]
    out_shape = [jax.ShapeDtypeStruct((m, n), out_dtype)]
    if emit_norm:
        out_specs += [pl.BlockSpec((tm, tn), lambda i, j, k: (i, j)),
                      pl.BlockSpec((tm, LANES), lambda i, j, k: (i, 0))]
        out_shape += [jax.ShapeDtypeStruct((m, n), BF16), jax.ShapeDtypeStruct((m, LANES), F32)]
    scratch = [pltpu.VMEM((tm, tn), F32)] if nk > 1 else []
    out = pl.pallas_call(
        functools.partial(_mm_kernel, nk=nk, kdim=kdim, two_a=a2 is not None, rowscale=ssq is not None,
                          epilogue=epilogue, emit_norm=emit_norm),
        grid=(m // tm, n // tn, nk),
        in_specs=in_specs,
        out_specs=out_specs,
        out_shape=out_shape,
        scratch_shapes=scratch,
        compiler_params=_params(("parallel", "arbitrary", "arbitrary")),
        name="matmul_" + (epilogue or "plain"),
    )(*args)
    return out if emit_norm else out[0]


CONV_TS = 256
CONV_HALO = 32
CONV_CW = 128


def _conv_kernel(a_ref, b_ref, w_ref, cb_ref, g_ref, bb_ref, o_ref, zbuf, ybuf):
    i = pl.program_id(0)
    ts = CONV_TS

    @pl.when(i == 0)
    def _():
        zbuf[0:CONV_HALO, :] = jnp.zeros((CONV_HALO, CONV_CH), F32)
        zbuf[CONV_HALO + ts:, :] = jnp.zeros((SUBLANES, CONV_CH), F32)

    @pl.when(i > 0)
    def _():
        zbuf[0:CONV_HALO, :] = zbuf[ts:ts + CONV_HALO, :]

    zbuf[CONV_HALO:CONV_HALO + ts, :] = a_ref[...] * jax.nn.sigmoid(b_ref[...])

    base = CONV_HALO - (CONV_K - 1)

    def chunk(c, carry):
        cols = pl.ds(pl.multiple_of(c * CONV_CW, CONV_CW), CONV_CW)
        acc = jnp.zeros((ts, CONV_CW), F32)
        for phase in range(SUBLANES):
            part = jnp.zeros((ts + SUBLANES, CONV_CW), F32)
            for k in range(CONV_K):
                if (base + k) % SUBLANES == phase:
                    off = base + k - phase
                    part = part + w_ref[k:k + 1, cols] * zbuf[off:off + ts + SUBLANES, cols]
            acc = acc + part[phase:phase + ts, :]
        ybuf[:, cols] = acc + cb_ref[:, cols]
        return carry

    lax.fori_loop(0, CONV_CH // CONV_CW, chunk, 0)

    y = ybuf[...]
    mu = jnp.mean(y, axis=-1, keepdims=True)
    yc = y - mu
    var = jnp.mean(yc * yc, axis=-1, keepdims=True)
    yn = yc * lax.rsqrt(var + EPS) * g_ref[...] + bb_ref[...]
    o_ref[...] = (yn * jax.nn.sigmoid(yn)).astype(o_ref.dtype)


def _conformer_conv(h, conv_w, conv_b, ln_g, ln_b):
    s = h.shape[0]
    c = CONV_CH
    wpad = jnp.zeros((CONV_HALO, c), F32).at[:CONV_K].set(conv_w)
    row = lambda v: v.reshape(1, c)
    full = lambda r: pl.BlockSpec((r, c), lambda i: (0, 0))
    return pl.pallas_call(
        _conv_kernel,
        grid=(s // CONV_TS,),
        in_specs=[pl.BlockSpec((CONV_TS, c), lambda i: (i, 0)),
                  pl.BlockSpec((CONV_TS, c), lambda i: (i, 1)),
                  full(CONV_HALO), full(1), full(1), full(1)],
        out_specs=pl.BlockSpec((CONV_TS, c), lambda i: (i, 0)),
        out_shape=jax.ShapeDtypeStruct((s, MIX_W), BF16),
        scratch_shapes=[pltpu.VMEM((CONV_TS + CONV_HALO + SUBLANES, c), F32),
                        pltpu.VMEM((CONV_TS, c), F32)],
        compiler_params=_params(("arbitrary",)),
        name="conformer_conv",
    )(h, h, wpad, row(conv_b), row(ln_g), row(ln_b))


XATTN_TS = 512


def _head_rmsnorm(x, gain, scale=1.0):
    ms = jnp.mean(x * x, axis=-1, keepdims=True)
    return x * (lax.rsqrt(ms + EPS) * scale) * gain


def _xattn_kernel(q_ref, kv_ref, qg_ref, kg_ref, o_ref):
    dh = X_HEAD_DIM
    for h in range(X_HEADS):
        q = _head_rmsnorm(q_ref[:, h * dh:(h + 1) * dh], qg_ref[...], dh ** -0.5 * LOG2E).astype(BF16)
        k = _head_rmsnorm(kv_ref[:, h * dh:(h + 1) * dh], kg_ref[...]).astype(BF16)
        v = kv_ref[:, X_W + h * dh:X_W + (h + 1) * dh].astype(BF16)
        s = lax.dot_general(q, k, _NT, preferred_element_type=F32)
        p = jnp.exp2(s - jnp.max(s, axis=-1, keepdims=True))
        l = jnp.sum(p, axis=-1, keepdims=True)
        o = jnp.dot(p.astype(BF16), v, preferred_element_type=F32) / l
        o_ref[:, h * dh:(h + 1) * dh] = o.astype(o_ref.dtype)


def _memory_xattn(h, qx_off, kv, q_gain, k_gain):
    s = h.shape[0]
    n_mem = kv.shape[0]
    qblk = qx_off // X_W
    return pl.pallas_call(
        _xattn_kernel,
        grid=(s // XATTN_TS,),
        in_specs=[pl.BlockSpec((XATTN_TS, X_W), lambda i: (i, qblk)),
                  pl.BlockSpec((n_mem, 2 * X_W), lambda i: (0, 0)),
                  pl.BlockSpec((1, X_HEAD_DIM), lambda i: (0, 0)),
                  pl.BlockSpec((1, X_HEAD_DIM), lambda i: (0, 0))],
        out_specs=pl.BlockSpec((XATTN_TS, X_W), lambda i: (i, 0)),
        out_shape=jax.ShapeDtypeStruct((s, X_W), BF16),
        compiler_params=_params(("parallel",)),
        name="memory_xattn",
    )(h, kv, q_gain.reshape(1, -1), k_gain.reshape(1, -1))


KVPREP_TS = 512


def _kvprep_kernel(x_ref, g_ref, o_ref):
    dh = NSA_HEAD_DIM
    for part in range(4):
        for g in range(NSA_KV_HEADS):
            lo = part * NSA_KVW + g * dh
            x = x_ref[:, lo:lo + dh]
            if part % 2 == 0:
                x = _head_rmsnorm(x, g_ref[part // 2:part // 2 + 1, :])
            o_ref[:, lo:lo + dh] = x.astype(o_ref.dtype)


def _nsa_kvprep(h, k_gain):
    s = h.shape[0]
    w = 4 * NSA_KVW
    blk = (NSA_QW + 2 * NSA_KVW) // w
    return pl.pallas_call(
        _kvprep_kernel,
        grid=(s // KVPREP_TS,),
        in_specs=[pl.BlockSpec((KVPREP_TS, w), lambda i: (i, blk)),
                  pl.BlockSpec((2, NSA_HEAD_DIM), lambda i: (0, 0))],
        out_specs=pl.BlockSpec((KVPREP_TS, w), lambda i: (i, 0)),
        out_shape=jax.ShapeDtypeStruct((s, w), BF16),
        compiler_params=_params(("parallel",)),
        name="nsa_kvprep",
    )(h, k_gain[1:3])


def _compress_kernel(c_ref, pa_ref, pb_ref, w1_ref, w2_ref, kg_ref, o_ref):
    half = CMP_STRIDE * NSA_HEAD_DIM
    c = c_ref[...]
    first = jnp.dot((c + pa_ref[...]).astype(BF16), w1_ref[0:half, :], preferred_element_type=F32)
    second = jnp.dot((c + pb_ref[...]).astype(BF16), w1_ref[half:2 * half, :], preferred_element_type=F32)
    nchunk = c.shape[0]
    hid = first + pltpu.roll(second, nchunk - 1, 0)
    hid = hid * jax.nn.sigmoid(hid)
    out = jnp.dot(hid.astype(BF16), w2_ref[...], preferred_element_type=F32)
    normed = _head_rmsnorm(out, kg_ref[...])
    is_key = pl.program_id(0) == 0
    o_ref[...] = jnp.where(is_key, normed, out).astype(o_ref.dtype)


def _nsa_compress(h, cmp_pos, cmp_w1, cmp_w2, k_gain):
    s = h.shape[0]
    g, dh = NSA_KV_HEADS, NSA_HEAD_DIM
    nchunk = s // CMP_STRIDE
    half = CMP_STRIDE * dh
    c = h[:, NSA_QW:NSA_QW + 2 * NSA_KVW].reshape(s, 2, g, dh).transpose(1, 2, 0, 3).reshape(2, g, nchunk, half)
    pos = cmp_pos.reshape(2, 2, 1, half)
    return pl.pallas_call(
        _compress_kernel,
        grid=(2, g),
        in_specs=[pl.BlockSpec((None, None, nchunk, half), lambda a, b: (a, b, 0, 0)),
                  pl.BlockSpec((None, None, 1, half), lambda a, b: (a, 0, 0, 0)),
                  pl.BlockSpec((None, None, 1, half), lambda a, b: (a, 1, 0, 0)),
                  pl.BlockSpec((None, 2 * half, CMP_HIDDEN), lambda a, b: (a, 0, 0)),
                  pl.BlockSpec((None, CMP_HIDDEN, dh), lambda a, b: (a, 0, 0)),
                  pl.BlockSpec((1, dh), lambda a, b: (0, 0))],
        out_specs=pl.BlockSpec((None, None, nchunk, dh), lambda a, b: (a, b, 0, 0)),
        out_shape=jax.ShapeDtypeStruct((2, g, nchunk, dh), BF16),
        compiler_params=_params(("parallel", "parallel")),
        name="nsa_compress",
    )(c, pos, pos, cmp_w1.astype(BF16), cmp_w2.astype(BF16), k_gain[0:1])


NSA_TQ = 256
NSA_TK = 1024
WIN_KEYS = WINDOW + NSA_TQ


def _softmax_parts(s_t, bias_t):
    s_t = s_t + bias_t
    m = jnp.maximum(jnp.max(s_t, axis=0, keepdims=True), M_FLOOR)
    p = jnp.exp2(s_t - m)
    return p, jnp.sum(p, axis=0, keepdims=True)


def _safe_inv(l):
    return jnp.where(l > 0.0, 1.0 / l, 0.0)


def _top_n_membership(score_t):
    nblk = score_t.shape[0]
    j_col = lax.broadcasted_iota(jnp.int32, (nblk, 1), 0).astype(F32)
    sel = jnp.zeros(score_t.shape, F32)
    s = score_t
    for _ in range(N_SEL):
        top = jnp.max(s, axis=0, keepdims=True)
        first = jnp.min(jnp.where(s == top, j_col, float(nblk)), axis=0, keepdims=True)
        hit = j_col == first
        sel = jnp.where(hit, 1.0, sel)
        s = jnp.where(hit, -jnp.inf, s)
    return sel


def _nsa_kernel(q_ref, gate_ref, kc_ref, vct_ref, ks_ref, vst_ref, kw_ref, vwt_ref, qg_ref, oh_ref, o_ref,
                selt_ref):
    r_, tq, dh, tk = NSA_GROUP, NSA_TQ, NSA_HEAD_DIM, NSA_TK
    i = pl.program_id(1)
    start = i * tq
    ncmp = kc_ref.shape[0]
    nslc = selt_ref.shape[0]
    t_row = start + lax.broadcasted_iota(jnp.int32, (1, tq), 1)
    head = lambda x, r: x[:, r * tq:(r + 1) * tq]
    heads = lambda f: jnp.concatenate([f(r) for r in range(r_)], axis=1)

    q_t = heads(lambda r: _head_rmsnorm(q_ref[:, r * dh:(r + 1) * dh], qg_ref[...],
                                        dh ** -0.5 * LOG2E).T.astype(BF16))

    n_col = lax.broadcasted_iota(jnp.int32, (ncmp, 1), 0)
    bias_c = jnp.where(n_col * CMP_STRIDE + (CMP_LEN - 1) <= t_row, 0.0, NEG)
    s_t = jnp.dot(kc_ref[...], q_t, preferred_element_type=F32)
    p_c = []
    for r in range(r_):
        p, l = _softmax_parts(head(s_t, r), bias_c)
        p_c.append(p * _safe_inv(l))
    o_c = jnp.dot(vct_ref[...], heads(lambda r: p_c[r].astype(BF16)), preferred_element_type=F32)

    psum = functools.reduce(jnp.add, p_c)
    n_row = lax.broadcasted_iota(jnp.int32, (1, ncmp), 1) * CMP_STRIDE
    j_col = lax.broadcasted_iota(jnp.int32, (nslc, 1), 0)
    overlap = jnp.where((n_row < (j_col + 1) * SEL_BLOCK) & (n_row + CMP_LEN > j_col * SEL_BLOCK),
                        1.0, 0.0).astype(BF16)
    p_hi = psum.astype(BF16)
    p_lo = (psum - p_hi.astype(F32)).astype(BF16)
    imp = (jnp.dot(overlap, p_hi, preferred_element_type=F32)
           + jnp.dot(overlap, p_lo, preferred_element_type=F32))
    cur = t_row >> 6
    valid = j_col * SEL_BLOCK <= t_row
    forced = (j_col == 0) | (j_col == cur) | (j_col == cur - 1)
    score = jnp.where(valid, imp + jnp.where(forced, FORCE, 0.0), NEG)
    selt_ref[...] = _top_n_membership(score)

    lo = pl.multiple_of(jnp.maximum(start - WINDOW, 0), tq)
    kpos = lo + lax.broadcasted_iota(jnp.int32, (WIN_KEYS, 1), 0)
    bias_w = jnp.where((kpos <= t_row) & (kpos > t_row - WINDOW), 0.0, NEG)
    s_t = jnp.dot(kw_ref[pl.ds(lo, WIN_KEYS), :], q_t, preferred_element_type=F32)
    p_w, l_w = [], []
    for r in range(r_):
        p, l = _softmax_parts(head(s_t, r), bias_w)
        p_w.append(p.astype(BF16))
        l_w.append(l)
    p_w = jnp.concatenate(p_w, axis=1)
    blk0 = lo // tq
    o_w = functools.reduce(jnp.add, [
        jnp.dot(vwt_ref[blk0 + b], p_w[b * tq:(b + 1) * tq, :], preferred_element_type=F32)
        for b in range(WIN_KEYS // tq)]) * _safe_inv(jnp.concatenate(l_w, axis=1))

    blocks_per_tile = tk // SEL_BLOCK
    n_tiles = (start + tq + tk - 1) // tk
    cat = lambda xs: jnp.concatenate(xs, axis=1)
    pair_w = 2 * tq
    n_pair = r_ // 2
    pair = lambda hp: slice(hp * pair_w, (hp + 1) * pair_w)
    vrows = vst_ref.shape[1]

    q_pad = jnp.zeros((LANES - blocks_per_tile, r_ * tq), BF16)

    def tile_step(c, carry, diagonal):
        m, acc = carry
        member = selt_ref[pl.ds(pl.multiple_of(c * blocks_per_tile, blocks_per_tile), blocks_per_tile), :]
        q_aug = jnp.concatenate(
            [q_t, cat([jnp.where(member > 0.5, 0.0, NEG).astype(BF16)] * r_), q_pad], axis=0)
        k_aug = jnp.concatenate([ks_ref[pl.ds(pl.multiple_of(c * tk, tk), tk), :], oh_ref[...]], axis=1)
        v_tile = vst_ref[c]
        scores = [jnp.dot(k_aug, q_aug[:, pair(hp)], preferred_element_type=F32) for hp in range(n_pair)]
        if diagonal:
            k_col = c * tk + lax.broadcasted_iota(jnp.int32, (tk, 1), 0)
            causal = jnp.where(k_col <= t_row, 0.0, NEG)
        m_out, acc_out = [], []
        for hp in range(n_pair):
            m_new, alpha, p_t = [], [], []
            for u in range(2):
                s = head(scores[hp], u)
                if diagonal:
                    s = s + causal
                m_u = jnp.maximum(head(m[hp], u), jnp.max(s, axis=0, keepdims=True))
                m_new.append(m_u)
                alpha.append(jnp.exp2(head(m[hp], u) - m_u))
                p_t.append(jnp.exp2(s - m_u).astype(BF16))
            m_out.append(cat(m_new))
            acc_out.append(cat(alpha) * acc[hp] + jnp.dot(v_tile, cat(p_t), preferred_element_type=F32))
        return tuple(m_out), tuple(acc_out)

    per_pair = lambda shape, v: tuple(jnp.full(shape, v, F32) for _ in range(n_pair))
    init = (per_pair((1, pair_w), M_FLOOR), per_pair((vrows, pair_w), 0.0))
    carry = lax.fori_loop(0, n_tiles - 1, functools.partial(tile_step, diagonal=False), init)
    _, acc_s = tile_step(n_tiles - 1, carry, diagonal=True)
    acc_s = cat(acc_s)
    o_s = acc_s[0:dh, :] * _safe_inv(acc_s[dh:dh + 1, :])

    gates = jax.nn.sigmoid(gate_ref[...]).T
    for r in range(r_):
        gate = lambda b: gates[b * GATE_STRIDE + r:b * GATE_STRIDE + r + 1, :]
        o = gate(0) * head(o_c, r) + gate(1) * head(o_s, r) + gate(2) * head(o_w, r)
        o_ref[:, r * dh:(r + 1) * dh] = o.T.astype(o_ref.dtype)


def _nsa_attention(h, h_tail, cmp_kv, kvn, q_gain):
    s = h.shape[0]
    g, dh = NSA_KV_HEADS, NSA_HEAD_DIM
    qw = NSA_GROUP * dh
    ncmp = cmp_kv.shape[2]
    gate_blk = X_W // LANES
    tk, tq = NSA_TK, NSA_TQ
    val_t = lambda part, blk: (kvn[:, part * NSA_KVW:(part + 1) * NSA_KVW]
                               .reshape(s // blk, blk, g, dh).transpose(2, 0, 3, 1))
    vc_t = jnp.swapaxes(cmp_kv[1], 1, 2)
    vs_t = val_t(1, tk)
    ones_rows = jnp.zeros((g, s // tk, BF16_SUBLANES, tk), BF16).at[:, :, 0, :].set(1.0)
    vs_t = jnp.concatenate([vs_t, ones_rows], axis=2)
    vrows = dh + BF16_SUBLANES
    block_onehot = (lax.broadcasted_iota(jnp.int32, (tk, LANES), 0) // SEL_BLOCK
                    == lax.broadcasted_iota(jnp.int32, (tk, LANES), 1)).astype(BF16)
    k_spec = lambda part: pl.BlockSpec((s, dh), lambda a, i: (0, part * g + a))
    v_spec = lambda rows, blk: pl.BlockSpec((None, s // blk, rows, blk), lambda a, i: (a, 0, 0, 0))
    return pl.pallas_call(
        _nsa_kernel,
        grid=(g, s // tq),
        in_specs=[pl.BlockSpec((tq, qw), lambda a, i: (i, a)),
                  pl.BlockSpec((tq, LANES), lambda a, i: (i, gate_blk + a)),
                  pl.BlockSpec((None, None, ncmp, dh), lambda a, i: (0, a, 0, 0)),
                  pl.BlockSpec((None, dh, ncmp), lambda a, i: (a, 0, 0)),
                  k_spec(0), v_spec(vrows, tk), k_spec(2), v_spec(dh, tq),
                  pl.BlockSpec((1, dh), lambda a, i: (0, 0)),
                  pl.BlockSpec((tk, LANES), lambda a, i: (0, 0))],
        out_specs=pl.BlockSpec((tq, qw), lambda a, i: (i, a)),
        out_shape=jax.ShapeDtypeStruct((s, MIX_W), BF16),
        scratch_shapes=[pltpu.VMEM((s // SEL_BLOCK, tq), F32)],
        compiler_params=_params(("parallel", "arbitrary")),
        name="nsa_attention",
    )(h, h_tail, cmp_kv, vc_t, kvn, vs_t, kvn, val_t(3, tq), q_gain.reshape(1, dh), block_onehot)


def _nsa_tail_weight(w_tail):
    k = w_tail.shape[0]
    gate_cols = jnp.zeros((k, NSA_KV_HEADS, LANES), w_tail.dtype)
    gsrc = w_tail[:, :NSA_GW].reshape(k, NSA_KV_HEADS, NSA_GROUP, 3)
    for b in range(3):
        gate_cols = gate_cols.at[:, :, b * GATE_STRIDE:b * GATE_STRIDE + NSA_GROUP].set(gsrc[..., b])
    return jnp.concatenate([w_tail[:, NSA_GW:], gate_cols.reshape(k, -1)], axis=1)


MM_TM = 1024
NORM_TM = 256


def kernel(x, mem, attn_norm, mem_norm, w_mem_kv, xq_gain, xk_gain, w_out, mlp_norm, w_up, w_down,
           cf_w_in, cf_conv_w, cf_conv_b, cf_ln_g, cf_ln_b,
           nsa_w_in, nsa_q_gain, nsa_k_gain, nsa_cmp_pos, nsa_cmp_w1, nsa_cmp_w2):
    b, s, d = x.shape
    assert b == 1 and d == D_MODEL and s % MM_TM == 0 and s // SEL_BLOCK == LANES
    x2 = x.reshape(s, d)
    mem2 = mem.reshape(mem.shape[1], d)
    n_mem = mem2.shape[0]
    mm = functools.partial(_matmul, tm=MM_TM, tk=d)
    down_w = w_down.astype(BF16)
    xg, ssq = _prenorm(x2, attn_norm[0], NORM_TM)
    for i in range(DEPTH):
        j = i // 2
        if i % 2 == 0:
            h = mm(xg, cf_w_in, layer=j, tn=512, out_dtype=F32, ssq=ssq)
            mix = _conformer_conv(h, cf_conv_w[j], cf_conv_b[j], cf_ln_g[j], cf_ln_b[j])
            h_x, qx_off = h, CF_QX_OFF
        else:
            h = mm(xg, nsa_w_in, layer=j, tn=512, n=NSA_MAIN_COLS, out_dtype=F32, ssq=ssq)
            h_x = mm(xg, _nsa_tail_weight(nsa_w_in[j, :, NSA_MAIN_COLS:]), tn=512, out_dtype=F32, ssq=ssq)
            cmp_kv = _nsa_compress(h, nsa_cmp_pos[j], nsa_cmp_w1[j], nsa_cmp_w2[j], nsa_k_gain[j])
            kvn = _nsa_kvprep(h, nsa_k_gain[j])
            mix = _nsa_attention(h, h_x, cmp_kv, kvn, nsa_q_gain[j])
            qx_off = 0
        memg, mem_ssq = _prenorm(mem2, mem_norm[i], n_mem)
        kv = _matmul(memg, w_mem_kv, layer=i, tm=n_mem, tn=1024, tk=d, out_dtype=F32, ssq=mem_ssq)
        xo = _memory_xattn(h_x, qx_off, kv, xq_gain[i], xk_gain[i])
        x2, xg, ssq = mm(mix, w_out, layer=i, a2=xo, tn=512, out_dtype=F32,
                         epilogue="residual", residual=x2, next_gain=mlp_norm[i])
        u = mm(xg, w_up, layer=i, tn=512, out_dtype=BF16, ssq=ssq, epilogue="relu2")
        down = functools.partial(_matmul, u, down_w, layer=i, tm=512, tn=256, tk=D_FF, out_dtype=F32,
                                 epilogue="residual", residual=x2)
        if i + 1 < DEPTH:
            x2, xg, ssq = down(next_gain=attn_norm[i + 1])
        else:
            x2 = down()
    return x2.reshape(b, s, d)
```
